```python
import math
import jax, jax.numpy as jnp
from jax import lax
import numpy as np

D_MODEL = 1024
BATCH = 1
SEQ = 16384
DEPTH = 1
DEC_BATCH = 128
DEC_SEQ = 4
PAST_LEN = 16384
PAGE_SIZE = 128

MOBA_HEADS = 8
MOBA_HEAD_DIM = 64
MOBA_BLOCK = 256
MOBA_TOPK = 3
MOBA_Q_CHUNK = 64
MLA_HEADS = 8
MLA_Q_RANK = 256
MLA_KV_RANK = 128
MLA_NOPE_DIM = 64
MLA_ROPE_DIM = 32
MLA_V_DIM = 64
ROPE_THETA = 10000.0
ATTN_Q_BLOCK = 128
MOBA_WIDTH = MOBA_HEADS * MOBA_HEAD_DIM
MLA_WIDTH = MLA_HEADS * MLA_V_DIM
MIX_WIDTH = MOBA_WIDTH + MLA_WIDTH
IN_COLS = 3 * MOBA_WIDTH + MLA_Q_RANK + MLA_KV_RANK + MLA_ROPE_DIM
N_EXPERTS = 64
EXPERT_DIM = 256
TOP_K = 8
N_GROUPS = 8
TOPK_GROUPS = 4
SHARED_DIM = 256
ROUTED_SCALE = 2.5
MOE_BLOCK_ROWS = 256
DEEPNORM_ALPHA = (2 * DEPTH) ** 0.25
DEEPNORM_BETA = (8 * DEPTH) ** -0.25
LN_EPS = 1e-5
RMS_EPS = 1e-6

kernel_name = 'hybrid_moba_mla_moe_adaln_decode_step'


def ln_plain(x):
    xf = x.astype(jnp.float32)
    mu = jnp.mean(xf, -1, keepdims=True)
    var = jnp.mean(jnp.square(xf - mu), -1, keepdims=True)
    return ((xf - mu) * lax.rsqrt(var + LN_EPS)).astype(x.dtype)


def layer_norm(x, g, b):
    xf = x.astype(jnp.float32)
    mu = jnp.mean(xf, -1, keepdims=True)
    var = jnp.mean(jnp.square(xf - mu), -1, keepdims=True)
    y = (xf - mu) * lax.rsqrt(var + LN_EPS) * g.astype(jnp.float32) + b.astype(jnp.float32)
    return y.astype(x.dtype)


def rms_norm(x, g):
    xf = x.astype(jnp.float32)
    y = xf * lax.rsqrt(jnp.mean(jnp.square(xf), -1, keepdims=True) + RMS_EPS) * g.astype(jnp.float32)
    return y.astype(x.dtype)


def alibi_slopes():
    return jnp.exp2(-8.0 * jnp.arange(1, MOBA_HEADS + 1, dtype=jnp.float32) / MOBA_HEADS)


def rope(x, pos):
    p = x.shape[-1]
    half = p // 2
    inv = jnp.exp(-math.log(ROPE_THETA) * jnp.arange(half, dtype=jnp.float32) * (2.0 / p))
    ang = pos.astype(jnp.float32)[:, None] * inv[None, :]
    shape = (ang.shape[0],) + (1,) * (x.ndim - 3) + (half,)
    cos = jnp.cos(ang).reshape(shape)
    sin = jnp.sin(ang).reshape(shape)
    xf = x.astype(jnp.float32)
    x1, x2 = xf[..., :half], xf[..., half:]
    return jnp.concatenate([x1 * cos - x2 * sin, x2 * cos + x1 * sin], -1).astype(x.dtype)


def adaln(c, w_ada, b_ada):
    a = (jax.nn.silu(c) @ w_ada + b_ada)[:, None, :]
    return jnp.split(a, 6, axis=-1)


def modulate(x, shift, scale):
    return ln_plain(x) * (1.0 + scale) + shift


def mixer_inputs(h, pos, w_in, q_norm_g, kv_norm_g, w_uq):
    b, t, _ = h.shape
    u = h @ w_in
    o1, o2, o3 = MOBA_WIDTH, 2 * MOBA_WIDTH, 3 * MOBA_WIDTH
    o4 = o3 + MLA_Q_RANK
    o5 = o4 + MLA_KV_RANK
    q_a = u[..., :o1].reshape(b, t, MOBA_HEADS, MOBA_HEAD_DIM)
    k_a = u[..., o1:o2].reshape(b, t, MOBA_HEADS, MOBA_HEAD_DIM)
    v_a = u[..., o2:o3].reshape(b, t, MOBA_HEADS, MOBA_HEAD_DIM)
    cq = rms_norm(u[..., o3:o4], q_norm_g)
    q = (cq @ w_uq).reshape(b, t, MLA_HEADS, MLA_NOPE_DIM + MLA_ROPE_DIM)
    q_nope = q[..., :MLA_NOPE_DIM]
    q_rope = rope(q[..., MLA_NOPE_DIM:], pos)
    ckv = rms_norm(u[..., o4:o5], kv_norm_g)
    k_rope = rope(u[..., o5:], pos)
    return q_a, k_a, v_a, q_nope, q_rope, ckv, k_rope


def moba_prompt(q, k, v):
    b, s, h, dh = q.shape
    nb = -(-s // MOBA_BLOCK)
    pad = ((0, 0), (0, nb * MOBA_BLOCK - s), (0, 0), (0, 0))
    kb = jnp.pad(k, pad).reshape(b, nb, MOBA_BLOCK, h, dh)
    vb = jnp.pad(v, pad).reshape(b, nb, MOBA_BLOCK, h, dh)
    k_mean = jnp.mean(kb.astype(jnp.float32), axis=2)
    kbt = kb.transpose(0, 3, 1, 2, 4)
    vbt = vb.transpose(0, 3, 1, 2, 4)
    n_sel = min(MOBA_TOPK, nb - 1)
    scale = dh ** -0.5
    slopes = alibi_slopes()
    bi = jnp.arange(b)[:, None, None, None]
    hi = jnp.arange(h)[None, :, None, None]

    def chunk(i):
        t0 = i * MOBA_Q_CHUNK
        q_i = lax.dynamic_slice_in_dim(q, t0, MOBA_Q_CHUNK, 1)
        qpos = t0 + jnp.arange(MOBA_Q_CHUNK)
        b_own = t0 // MOBA_BLOCK
        k_own = lax.dynamic_index_in_dim(kb, b_own, 1, keepdims=False)
        v_own = lax.dynamic_index_in_dim(vb, b_own, 1, keepdims=False)
        kpos_own = b_own * MOBA_BLOCK + jnp.arange(MOBA_BLOCK)
        dist_own = (qpos[:, None] - kpos_own[None, :]).astype(jnp.float32)
        s_own = jnp.einsum('bqhd,bjhd->bhqj', q_i, k_own).astype(jnp.float32) * scale - slopes[:, None, None] * dist_own
        s_own = jnp.where(dist_own >= 0, s_own, -jnp.inf)
        if n_sel == 0:
            p = jax.nn.softmax(s_own, -1).astype(v.dtype)
            return jnp.einsum('bhqj,bjhd->bqhd', p, v_own)
        gate = jnp.einsum('bqhd,bnhd->bhqn', q_i.astype(jnp.float32), k_mean)
        gate = jnp.where(jnp.arange(nb) < b_own, gate, -jnp.inf)
        _, sel = lax.top_k(gate, n_sel)
        valid = sel < b_own
        k_sel = kbt[bi, hi, sel]
        v_sel = vbt[bi, hi, sel]
        kpos_sel = sel[..., None] * MOBA_BLOCK + jnp.arange(MOBA_BLOCK)
        dist_sel = (qpos[None, None, :, None, None] - kpos_sel).astype(jnp.float32)
        s_sel = jnp.einsum('bhqd,bhqnjd->bhqnj', q_i.transpose(0, 2, 1, 3), k_sel).astype(jnp.float32) * scale
        s_sel = s_sel - slopes[None, :, None, None, None] * dist_sel
        s_sel = jnp.where(valid[..., None], s_sel, -jnp.inf).reshape(b, h, MOBA_Q_CHUNK, n_sel * MOBA_BLOCK)
        p = jax.nn.softmax(jnp.concatenate([s_sel, s_own], -1), -1).astype(v.dtype)
        p_sel = p[..., :n_sel * MOBA_BLOCK].reshape(b, h, MOBA_Q_CHUNK, n_sel, MOBA_BLOCK)
        p_own = p[..., n_sel * MOBA_BLOCK:]
        return (jnp.einsum('bhqnj,bhqnjd->bqhd', p_sel, v_sel)
                + jnp.einsum('bhqj,bjhd->bqhd', p_own, v_own))

    out = lax.map(chunk, jnp.arange(s // MOBA_Q_CHUNK))
    return out.transpose(1, 0, 2, 3, 4).reshape(b, s, h * dh)


def moba_sample(q, k_new, v_new, k_pool, v_pool, page_table, layer):
    db, t, h, dh = q.shape
    past = page_table.shape[1] * k_pool.shape[2]
    n_full = past // MOBA_BLOCK
    tail = past - n_full * MOBA_BLOCK
    n_sel = min(MOBA_TOPK, n_full)
    scale = dh ** -0.5
    slopes = alibi_slopes()
    qpos = past + jnp.arange(t)
    kpos_own = n_full * MOBA_BLOCK + jnp.arange(tail + t)
    dist_own = (qpos[:, None] - kpos_own[None, :]).astype(jnp.float32)
    hi = jnp.arange(h)[:, None, None]

    def one(args):
        q_b, k_b, v_b, pages = args
        k_past = k_pool[layer, pages].reshape(past, h, dh)
        v_past = v_pool[layer, pages].reshape(past, h, dh)
        k_own = jnp.concatenate([k_past[n_full * MOBA_BLOCK:], k_b], 0)
        v_own = jnp.concatenate([v_past[n_full * MOBA_BLOCK:], v_b], 0)
        s_own = jnp.einsum('thd,jhd->htj', q_b, k_own).astype(jnp.float32) * scale - slopes[:, None, None] * dist_own
        s_own = jnp.where(dist_own >= 0, s_own, -jnp.inf)
        if n_sel == 0:
            p = jax.nn.softmax(s_own, -1).astype(v_own.dtype)
            return jnp.einsum('htj,jhd->thd', p, v_own)
        kb = k_past[:n_full * MOBA_BLOCK].reshape(n_full, MOBA_BLOCK, h, dh)
        vb = v_past[:n_full * MOBA_BLOCK].reshape(n_full, MOBA_BLOCK, h, dh)
        k_mean = jnp.mean(kb.astype(jnp.float32), axis=1)
        gate = jnp.einsum('thd,nhd->htn', q_b.astype(jnp.float32), k_mean)
        _, sel = lax.top_k(gate, n_sel)
        k_sel = kb.transpose(2, 0, 1, 3)[hi, sel]
        v_sel = vb.transpose(2, 0, 1, 3)[hi, sel]
        kpos_sel = sel[..., None] * MOBA_BLOCK + jnp.arange(MOBA_BLOCK)
        dist_sel = (qpos[None, :, None, None] - kpos_sel).astype(jnp.float32)
        s_sel = jnp.einsum('htd,htnjd->htnj', q_b.transpose(1, 0, 2), k_sel).astype(jnp.float32) * scale
        s_sel = (s_sel - slopes[:, None, None, None] * dist_sel).reshape(h, t, n_sel * MOBA_BLOCK)
        p = jax.nn.softmax(jnp.concatenate([s_sel, s_own], -1), -1).astype(v_own.dtype)
        p_sel = p[..., :n_sel * MOBA_BLOCK].reshape(h, t, n_sel, MOBA_BLOCK)
        p_own = p[..., n_sel * MOBA_BLOCK:]
        return (jnp.einsum('htnj,htnjd->thd', p_sel, v_sel)
                + jnp.einsum('htj,jhd->thd', p_own, v_own))

    out = lax.map(one, (q, k_new, v_new, page_table))
    return out.reshape(db, t, h * dh)


def mla_prompt(q_nope, q_rope, ckv, k_rope, w_uk, w_uv):
    b, s, h, _ = q_nope.shape
    k_nope = jnp.einsum('bsr,rhn->bshn', ckv, w_uk)
    v = jnp.einsum('bsr,rhv->bshv', ckv, w_uv)
    scale = (MLA_NOPE_DIM + MLA_ROPE_DIM) ** -0.5
    kpos = jnp.arange(s)

    def blk(i):
        t0 = i * ATTN_Q_BLOCK
        qn = lax.dynamic_slice_in_dim(q_nope, t0, ATTN_Q_BLOCK, 1)
        qr = lax.dynamic_slice_in_dim(q_rope, t0, ATTN_Q_BLOCK, 1)
        sc = (jnp.einsum('bqhn,bshn->bhqs', qn, k_nope)
              + jnp.einsum('bqhp,bsp->bhqs', qr, k_rope)).astype(jnp.float32) * scale
        qpos = t0 + jnp.arange(ATTN_Q_BLOCK)
        sc = jnp.where(kpos[None, :] <= qpos[:, None], sc, -jnp.inf)
        p = jax.nn.softmax(sc, -1).astype(v.dtype)
        return jnp.einsum('bhqs,bshv->bqhv', p, v)

    out = lax.map(blk, jnp.arange(s // ATTN_Q_BLOCK))
    return out.transpose(1, 0, 2, 3, 4).reshape(b, s, h * MLA_V_DIM)


def mla_sample(q_nope, q_rope, ckv_new, kr_new, ckv_pool, kr_pool, page_table, layer, w_uk, w_uv):
    db, t, h, _ = q_nope.shape
    past = page_table.shape[1] * ckv_pool.shape[2]
    ckv_all = jnp.concatenate([ckv_pool[layer, page_table].reshape(db, past, MLA_KV_RANK), ckv_new], 1)
    kr_all = jnp.concatenate([kr_pool[layer, page_table].reshape(db, past, MLA_ROPE_DIM), kr_new], 1)
    scale = (MLA_NOPE_DIM + MLA_ROPE_DIM) ** -0.5
    q_lat = jnp.einsum('bthn,rhn->bthr', q_nope, w_uk)
    sc = (jnp.einsum('bthr,bsr->bhts', q_lat, ckv_all)
          + jnp.einsum('bthp,bsp->bhts', q_rope, kr_all)).astype(jnp.float32) * scale
    kpos = jnp.arange(past + t)
    qpos = past + jnp.arange(t)
    sc = jnp.where(kpos[None, :] <= qpos[:, None], sc, -jnp.inf)
    p = jax.nn.softmax(sc, -1).astype(ckv_all.dtype)
    o_lat = jnp.einsum('bhts,bsr->bthr', p, ckv_all)
    return jnp.einsum('bthr,rhv->bthv', o_lat, w_uv).reshape(db, t, h * MLA_V_DIM)


def route(h, w_router, b_router):
    m = h.shape[0]
    scores = jax.nn.sigmoid((h @ w_router).astype(jnp.float32))
    sel = scores + b_router.astype(jnp.float32)
    grp = sel.reshape(m, N_GROUPS, N_EXPERTS // N_GROUPS)
    grp_score = jnp.sum(lax.top_k(grp, 2)[0], -1)
    _, gidx = lax.top_k(grp_score, TOPK_GROUPS)
    gmask = jnp.any(gidx[..., None] == jnp.arange(N_GROUPS), axis=-2)
    emask = jnp.repeat(gmask, N_EXPERTS // N_GROUPS, axis=-1)
    _, idx = lax.top_k(jnp.where(emask, sel, -jnp.inf), TOP_K)
    w = jnp.take_along_axis(scores, idx, -1)
    w = w / jnp.sum(w, -1, keepdims=True) * ROUTED_SCALE
    return idx, w


def swiglu(x, wg, wu, wd):
    return (jax.nn.silu(x @ wg) * (x @ wu)) @ wd


def moe_routed(h, idx, w, w_gate, w_up, w_down):
    m, d = h.shape
    a = m * TOP_K
    r = MOE_BLOCK_ROWS
    flat_e = idx.reshape(a)
    flat_tok = jnp.repeat(jnp.arange(m, dtype=jnp.int32), TOP_K)
    order = jnp.argsort(flat_e)
    e_sorted = flat_e[order]
    tok_sorted = flat_tok[order]
    w_sorted = w.reshape(a)[order]
    counts = jnp.bincount(flat_e, length=N_EXPERTS)
    padded = (counts + r - 1) // r * r
    pad_end = jnp.cumsum(padded)
    pad_start = pad_end - padded
    cnt_start = jnp.cumsum(counts) - counts
    dest = pad_start[e_sorted] + (jnp.arange(a) - cnt_start[e_sorted])
    n_blocks = -(-a // r) + N_EXPERTS
    slot_tok = jnp.full((n_blocks * r,), m, jnp.int32).at[dest].set(tok_sorted)
    block_e = jnp.minimum(jnp.searchsorted(pad_end, jnp.arange(n_blocks) * r, side='right'), N_EXPERTS - 1)
    h_pad = jnp.concatenate([h, jnp.zeros((1, d), h.dtype)], 0)

    def run(args):
        toks, e = args
        return swiglu(h_pad[toks], w_gate[e], w_up[e], w_down[e])

    y = lax.map(run, (slot_tok.reshape(n_blocks, r), block_e)).reshape(n_blocks * r, d)
    y_assign = y[dest] * w_sorted[:, None].astype(y.dtype)
    return jax.ops.segment_sum(y_assign, tok_sorted, num_segments=m)


def moe_ffn(h, w_router, b_router, w_gate, w_up, w_down, ws_gate, ws_up, ws_down):
    b, t, d = h.shape
    hf = h.reshape(b * t, d)
    idx, w = route(hf, w_router, b_router)
    y = moe_routed(hf, idx, w, w_gate, w_up, w_down) + swiglu(hf, ws_gate, ws_up, ws_down)
    return y.reshape(b, t, d)


def setup_inputs(seed: int = 0) -> dict:
    key = jax.random.key(seed)
    ks = iter(jax.random.split(key, 40))
    f32 = jnp.float32
    n_pages = PAST_LEN // PAGE_SIZE
    n_used = DEC_BATCH * n_pages
    n_pool = n_used + n_used // 4

    def nrm(shape, scale):
        return jax.random.normal(next(ks), shape, f32) * scale

    def gain(shape):
        return 1.0 + nrm(shape, 0.02)

    d = D_MODEL
    x_prompt = jax.random.normal(next(ks), (BATCH, SEQ, d), f32)
    x_sample = jax.random.normal(next(ks), (DEC_BATCH, DEC_SEQ, d), f32)
    cache_moba_k = jax.random.normal(next(ks), (DEPTH, n_pool, PAGE_SIZE, MOBA_HEADS, MOBA_HEAD_DIM), f32)
    cache_moba_v = jax.random.normal(next(ks), (DEPTH, n_pool, PAGE_SIZE, MOBA_HEADS, MOBA_HEAD_DIM), f32)
    cache_mla_ckv = jax.random.normal(next(ks), (DEPTH, n_pool, PAGE_SIZE, MLA_KV_RANK), f32)
    cache_mla_krope = jax.random.normal(next(ks), (DEPTH, n_pool, PAGE_SIZE, MLA_ROPE_DIM), f32)
    page_table = jax.random.permutation(next(ks), n_pool)[:n_used].reshape(DEC_BATCH, n_pages).astype(jnp.int32)
    c_prompt = jax.random.normal(next(ks), (BATCH, d), f32)
    c_sample = jax.random.normal(next(ks), (DEC_BATCH, d), f32)
    return {
        'x_prompt': x_prompt,
        'x_sample': x_sample,
        'cache_moba_k': cache_moba_k,
        'cache_moba_v': cache_moba_v,
        'cache_mla_ckv': cache_mla_ckv,
        'cache_mla_krope': cache_mla_krope,
        'page_table': page_table,
        'c_prompt': c_prompt,
        'c_sample': c_sample,
        'w_ada': nrm((DEPTH, d, 6 * d), 0.5 * d ** -0.5),
        'b_ada': nrm((DEPTH, 6 * d), 0.02),
        'w_in': nrm((DEPTH, d, IN_COLS), d ** -0.5),
        'q_norm_g': gain((DEPTH, MLA_Q_RANK)),
        'kv_norm_g': gain((DEPTH, MLA_KV_RANK)),
        'w_uq': nrm((DEPTH, MLA_Q_RANK, MLA_HEADS * (MLA_NOPE_DIM + MLA_ROPE_DIM)), MLA_Q_RANK ** -0.5),
        'w_uk': nrm((DEPTH, MLA_KV_RANK, MLA_HEADS, MLA_NOPE_DIM), MLA_KV_RANK ** -0.5),
        'w_uv': nrm((DEPTH, MLA_KV_RANK, MLA_HEADS, MLA_V_DIM), MLA_KV_RANK ** -0.5),
        'w_o': nrm((DEPTH, MIX_WIDTH, d), MIX_WIDTH ** -0.5 * DEEPNORM_BETA),
        'ln1_g': gain((DEPTH, d)),
        'ln1_b': nrm((DEPTH, d), 0.02),
        'w_router': nrm((DEPTH, d, N_EXPERTS), d ** -0.5),
        'b_router': nrm((DEPTH, N_EXPERTS), 0.01),
        'w_gate': nrm((DEPTH, N_EXPERTS, d, EXPERT_DIM), d ** -0.5),
        'w_up': nrm((DEPTH, N_EXPERTS, d, EXPERT_DIM), d ** -0.5),
        'w_down': nrm((DEPTH, N_EXPERTS, EXPERT_DIM, d), EXPERT_DIM ** -0.5 * DEEPNORM_BETA),
        'ws_gate': nrm((DEPTH, d, SHARED_DIM), d ** -0.5),
        'ws_up': nrm((DEPTH, d, SHARED_DIM), d ** -0.5),
        'ws_down': nrm((DEPTH, SHARED_DIM, d), SHARED_DIM ** -0.5 * DEEPNORM_BETA),
        'ln2_g': gain((DEPTH, d)),
        'ln2_b': nrm((DEPTH, d), 0.02),
    }


def reference(x_prompt, x_sample, cache_moba_k, cache_moba_v, cache_mla_ckv, cache_mla_krope, page_table,
              c_prompt, c_sample, w_ada, b_ada, w_in, q_norm_g, kv_norm_g, w_uq, w_uk, w_uv, w_o,
              ln1_g, ln1_b, w_router, b_router, w_gate, w_up, w_down, ws_gate, ws_up, ws_down, ln2_g, ln2_b):
    pos_p = jnp.arange(x_prompt.shape[1])
    pos_s = PAST_LEN + jnp.arange(x_sample.shape[1])

    def layer(l, x, c, pos, attend):
        sh1, sc1, g1, sh2, sc2, g2 = adaln(c, w_ada[l], b_ada[l])
        h = modulate(x, sh1, sc1)
        q_a, k_a, v_a, q_nope, q_rope, ckv, k_rope = mixer_inputs(h, pos, w_in[l], q_norm_g[l], kv_norm_g[l], w_uq[l])
        mix = attend(q_a, k_a, v_a, q_nope, q_rope, ckv, k_rope)
        x = layer_norm(DEEPNORM_ALPHA * x + g1 * (mix @ w_o[l]), ln1_g[l], ln1_b[l])
        h2 = modulate(x, sh2, sc2)
        f = moe_ffn(h2, w_router[l], b_router[l], w_gate[l], w_up[l], w_down[l], ws_gate[l], ws_up[l], ws_down[l])
        x = layer_norm(DEEPNORM_ALPHA * x + g2 * f, ln2_g[l], ln2_b[l])
        return x, k_a, v_a, ckv, k_rope

    xp, xs = x_prompt, x_sample
    kp_l, vp_l, cp_l, rp_l = [], [], [], []
    ks_l, vs_l, cs_l, rs_l = [], [], [], []
    for l in range(DEPTH):
        def attend_prompt(q_a, k_a, v_a, q_nope, q_rope, ckv, k_rope, l=l):
            return jnp.concatenate([moba_prompt(q_a, k_a, v_a),
                                    mla_prompt(q_nope, q_rope, ckv, k_rope, w_uk[l], w_uv[l])], -1)

        def attend_sample(q_a, k_a, v_a, q_nope, q_rope, ckv, k_rope, l=l):
            return jnp.concatenate([moba_sample(q_a, k_a, v_a, cache_moba_k, cache_moba_v, page_table, l),
                                    mla_sample(q_nope, q_rope, ckv, k_rope, cache_mla_ckv, cache_mla_krope,
                                               page_table, l, w_uk[l], w_uv[l])], -1)

        xp, kp, vp, cp, rp = layer(l, xp, c_prompt, pos_p, attend_prompt)
        xs, ks, vs, cs, rs = layer(l, xs, c_sample, pos_s, attend_sample)
        kp_l.append(kp); vp_l.append(vp); cp_l.append(cp); rp_l.append(rp)
        ks_l.append(ks); vs_l.append(vs); cs_l.append(cs); rs_l.append(rs)

    y_prompt, y_sample = xp, xs
    return (y_prompt, y_sample,
            jnp.stack(kp_l), jnp.stack(vp_l), jnp.stack(cp_l), jnp.stack(rp_l),
            jnp.stack(ks_l), jnp.stack(vs_l), jnp.stack(cs_l), jnp.stack(rs_l))
```

```python
import functools
import math

import jax
import jax.numpy as jnp
from jax import lax
from jax.experimental import pallas as pl
from jax.experimental.pallas import tpu as pltpu

F32 = jnp.float32
BF16 = jnp.bfloat16
I32 = jnp.int32

D_MODEL = 1024
MOBA_HEADS = 8
MOBA_HEAD_DIM = 64
MOBA_BLOCK = 256
MOBA_TOPK = 3
MLA_HEADS = 8
MLA_Q_RANK = 256
MLA_KV_RANK = 128
MLA_NOPE_DIM = 64
MLA_ROPE_DIM = 32
MLA_V_DIM = 64
ROPE_THETA = 10000.0
MOBA_WIDTH = MOBA_HEADS * MOBA_HEAD_DIM
MLA_WIDTH = MLA_HEADS * MLA_V_DIM
N_EXPERTS = 64
EXPERT_DIM = 256
TOP_K = 8
N_GROUPS = 8
TOPK_GROUPS = 4
SHARED_DIM = 256
ROUTED_SCALE = 2.5
DEPTH = 1
DEEPNORM_ALPHA = (2 * DEPTH) ** 0.25
LN_EPS = 1e-5
RMS_EPS = 1e-6

LANES = 128
HEAD_PAD = 128
MLA_QK_DIM = MLA_NOPE_DIM + MLA_ROPE_DIM
MOBA_SCALE = MOBA_HEAD_DIM ** -0.5
MLA_SCALE = MLA_QK_DIM ** -0.5
NEG_BIG = -1e30
VMEM_LIMIT = 56 * 1024 * 1024

TOKEN_TILE = 256
_C_Q, _C_K, _C_V = 0, MOBA_WIDTH, 2 * MOBA_WIDTH
_C_CQ = 3 * MOBA_WIDTH
_C_CKV = _C_CQ + MLA_Q_RANK
_C_KR = _C_CKV + MLA_KV_RANK
_C_KRS = _C_KR + LANES
IN_COLS_EXT = _C_KRS + LANES


def _cparams(sem, vmem=VMEM_LIMIT):
    return pltpu.CompilerParams(dimension_semantics=sem, vmem_limit_bytes=vmem)


def _ln_plain(x):
    mu = jnp.mean(x, -1, keepdims=True)
    xc = x - mu
    var = jnp.mean(xc * xc, -1, keepdims=True)
    return xc * lax.rsqrt(var + LN_EPS)


def _dot(a, b):
    return jnp.dot(a, b, preferred_element_type=F32)


def _dot_nt(a, b):
    return lax.dot_general(a, b, (((1,), (1,)), ((), ())), preferred_element_type=F32)


def _dot_tn(a, b):
    return lax.dot_general(a, b, (((0,), (0,)), ((), ())), preferred_element_type=F32)


def _scalar_f32(v):
    return jnp.asarray(v, I32).astype(F32)


def _split_bf16(x):
    hi = x.astype(BF16)
    lo = (x - hi.astype(F32)).astype(BF16)
    return hi, lo


def _ada_kernel(c_ref, w_ref, b_ref, o_ref):
    c = c_ref[...]
    s = c * jax.nn.sigmoid(c)
    o_ref[...] = _dot(s.astype(BF16), w_ref[...].astype(BF16)) + b_ref[...]


def _adaln(c, w_ada, b_ada):
    b, d = c.shape
    n = w_ada.shape[1]
    tn = 1024
    return pl.pallas_call(
        _ada_kernel,
        grid=(n // tn,),
        in_specs=[pl.BlockSpec((b, d), lambda j: (0, 0)),
                  pl.BlockSpec((d, tn), lambda j: (0, j)),
                  pl.BlockSpec((1, tn), lambda j: (0, j))],
        out_specs=pl.BlockSpec((b, tn), lambda j: (0, j)),
        out_shape=jax.ShapeDtypeStruct((b, n), F32),
        compiler_params=_cparams(("arbitrary",)),
        name="adaln",
    )(c, w_ada, b_ada.reshape(1, n))


def _in_kernel(x_ref, sh_ref, sc_ref, win_ref, qg_ref, kvg_ref, wqa_ref, wqb_ref, wuk_ref, wuv_ref,
               ta_ref, tb_ref, tkc_ref, tks_ref,
               qaf_ref, qab_ref, ka_ref, va_ref, kab_ref, vab_ref, ckv_ref, kr_ref,
               qm_ref, km_ref, vm_ref, kmean_ref):
    x = x_ref[...]
    h = _ln_plain(x) * (1.0 + sc_ref[...]) + sh_ref[...]
    u = _dot(h.astype(BF16), win_ref[...])
    q_a = u[:, _C_Q:_C_K]
    k_a = u[:, _C_K:_C_V]
    v_a = u[:, _C_V:_C_CQ]
    qaf_ref[...] = q_a
    qab_ref[...] = (q_a * MOBA_SCALE).astype(BF16)
    ka_ref[...] = k_a
    va_ref[...] = v_a
    kab_ref[...] = k_a.astype(BF16)
    vab_ref[...] = v_a.astype(BF16)
    kmean_ref[0] = jnp.mean(k_a, axis=0, keepdims=True)

    cq = u[:, _C_CQ:_C_CKV]
    cq = cq * lax.rsqrt(jnp.mean(cq * cq, -1, keepdims=True) + RMS_EPS) * qg_ref[...]
    cqb = cq.astype(BF16)
    qa = _dot(cqb, wqa_ref[...])
    qb = _dot(cqb, wqb_ref[...])
    ta = ta_ref[...]
    tb = tb_ref[...]
    for hh in range(MLA_HEADS):
        sl = slice(hh * HEAD_PAD, (hh + 1) * HEAD_PAD)
        qm_ref[:, sl] = (qa[:, sl] * ta + qb[:, sl] * tb).astype(BF16)

    ckv = u[:, _C_CKV:_C_KR]
    ckv = ckv * lax.rsqrt(jnp.mean(ckv * ckv, -1, keepdims=True) + RMS_EPS) * kvg_ref[...]
    ckv_ref[...] = ckv
    ckvb = ckv.astype(BF16)
    kr = u[:, _C_KR:_C_KRS] * tkc_ref[...] + u[:, _C_KRS:IN_COLS_EXT] * tks_ref[...]
    kr_ref[...] = pltpu.roll(kr, LANES - MLA_NOPE_DIM, 1)[:, :MLA_ROPE_DIM]
    kn = _dot(ckvb, wuk_ref[...])
    for hh in range(MLA_HEADS):
        sl = slice(hh * HEAD_PAD, (hh + 1) * HEAD_PAD)
        km_ref[:, sl] = (kn[:, sl] + kr).astype(BF16)
    vm_ref[...] = _dot(ckvb, wuv_ref[...]).astype(BF16)


def _rope_tables(pos):
    half = MLA_ROPE_DIM // 2
    inv = jnp.exp(-math.log(ROPE_THETA) * jnp.arange(half, dtype=F32) * (2.0 / MLA_ROPE_DIM))
    ang = pos.astype(F32)[:, None] * inv[None, :]
    cos = jnp.cos(ang)
    sin = jnp.sin(ang)
    cos2 = jnp.concatenate([cos, cos], -1)
    sin2 = jnp.concatenate([sin, sin], -1)
    t = pos.shape[0]
    z32 = jnp.zeros((t, MLA_ROPE_DIM), F32)
    z64 = jnp.zeros((t, MLA_NOPE_DIM), F32)
    ones = jnp.ones((t, MLA_NOPE_DIM), F32)
    ta = jnp.concatenate([ones, cos2, z32], -1) * MLA_SCALE
    tb = jnp.concatenate([z64, sin2, z32], -1) * MLA_SCALE
    tkc = jnp.concatenate([z64, cos2, z32], -1)
    tks = jnp.concatenate([z64, sin2, z32], -1)
    return ta, tb, tkc, tks


def _rot_half_cols(w):
    half = MLA_ROPE_DIM // 2
    return jnp.concatenate([-w[..., half:], w[..., :half]], -1)


def _prep_in_weights(w_in, w_uq, w_uk, w_uv):
    d = w_in.shape[0]
    o5 = 3 * MOBA_WIDTH + MLA_Q_RANK + MLA_KV_RANK
    w_kr = w_in[:, o5:]
    z64 = jnp.zeros((d, MLA_NOPE_DIM), F32)
    z32 = jnp.zeros((d, MLA_ROPE_DIM), F32)
    win_ext = jnp.concatenate([w_in[:, :o5], z64, w_kr, z32, z64, _rot_half_cols(w_kr), z32], -1).astype(BF16)
    wq = w_uq.reshape(MLA_Q_RANK, MLA_HEADS, MLA_QK_DIM)
    zq = jnp.zeros((MLA_Q_RANK, MLA_HEADS, MLA_ROPE_DIM), F32)
    zn = jnp.zeros((MLA_Q_RANK, MLA_HEADS, MLA_NOPE_DIM), F32)
    wqa = jnp.concatenate([wq, zq], -1).reshape(MLA_Q_RANK, MLA_HEADS * HEAD_PAD).astype(BF16)
    wqb = jnp.concatenate([zn, _rot_half_cols(wq[..., MLA_NOPE_DIM:]), zq], -1)
    wqb = wqb.reshape(MLA_Q_RANK, MLA_HEADS * HEAD_PAD).astype(BF16)
    zk = jnp.zeros((MLA_KV_RANK, MLA_HEADS, HEAD_PAD - MLA_NOPE_DIM), F32)
    wuk = jnp.concatenate([w_uk, zk], -1).reshape(MLA_KV_RANK, MLA_HEADS * HEAD_PAD).astype(BF16)
    wuv = w_uv.reshape(MLA_KV_RANK, MLA_WIDTH).astype(BF16)
    return win_ext, wqa, wqb, wuk, wuv


def _in_stage(x, sh, sc, pos, weights, q_norm_g, kv_norm_g):
    win_ext, wqa, wqb, wuk, wuv = weights
    m, d = x.shape
    tm = TOKEN_TILE
    nt = m // tm
    ta, tb, tkc, tks = _rope_tables(pos)
    row = lambda i: (i, 0)
    fix = lambda i: (0, 0)
    mod_spec = pl.BlockSpec((tm, d), row) if sh.shape[0] == m else pl.BlockSpec((1, d), fix)
    full = lambda a: pl.BlockSpec(a.shape, fix)
    hw = MLA_HEADS * HEAD_PAD
    outs = [
        ((m, MOBA_WIDTH), F32), ((m, MOBA_WIDTH), BF16),
        ((m, MOBA_WIDTH), F32), ((m, MOBA_WIDTH), F32),
        ((m, MOBA_WIDTH), BF16), ((m, MOBA_WIDTH), BF16),
        ((m, MLA_KV_RANK), F32), ((m, MLA_ROPE_DIM), F32),
        ((m, hw), BF16), ((m, hw), BF16), ((m, MLA_WIDTH), BF16),
    ]
    out_shape = [jax.ShapeDtypeStruct(s, t) for s, t in outs] + [jax.ShapeDtypeStruct((nt, 1, MOBA_WIDTH), F32)]
    out_specs = [pl.BlockSpec((tm, s[1]), row) for s, _ in outs] + [pl.BlockSpec((1, 1, MOBA_WIDTH), lambda i: (i, 0, 0))]
    return pl.pallas_call(
        _in_kernel,
        grid=(nt,),
        in_specs=[pl.BlockSpec((tm, d), row), mod_spec, mod_spec, full(win_ext),
                  pl.BlockSpec((1, MLA_Q_RANK), fix), pl.BlockSpec((1, MLA_KV_RANK), fix),
                  full(wqa), full(wqb), full(wuk), full(wuv),
                  pl.BlockSpec((tm, LANES), row), pl.BlockSpec((tm, LANES), row),
                  pl.BlockSpec((tm, LANES), row), pl.BlockSpec((tm, LANES), row)],
        out_specs=out_specs,
        out_shape=out_shape,
        compiler_params=_cparams(("arbitrary",)),
        name="in_stage",
    )(x, sh, sc, win_ext, q_norm_g.reshape(1, -1), kv_norm_g.reshape(1, -1), wqa, wqb, wuk, wuv, ta, tb, tkc, tks)


def _dot3(a, b):
    ah, al = _split_bf16(a)
    bh, bl = _split_bf16(b)
    return _dot(ah, bh) + (_dot(ah, bl) + _dot(al, bh))


MAX_BLOCKS = 64
N_PAIRS = MOBA_HEADS // 2
IDS_LANES = 8


def _top3_ids(g, lanef, base):
    ids = []
    for _ in range(MOBA_TOPK):
        m = jnp.max(g, axis=1, keepdims=True)
        idx = jnp.min(jnp.where(g == m, lanef, 1e9), axis=1, keepdims=True)
        ids.append(jnp.where(m > -jnp.inf, idx - base, -1.0))
        g = jnp.where(lanef == idx, -jnp.inf, g)
    return ids


def _gate_kernel(q_ref, km_ref, ids_ref):
    b_own = pl.program_id(0)
    gate = _dot3(q_ref[...], km_ref[...])
    t = gate.shape[0]
    lane = lax.broadcasted_iota(I32, (t, LANES), 1)
    lanef = lane.astype(F32)
    out_lane = lax.broadcasted_iota(I32, (t, IDS_LANES), 1)
    eligible = (lane & (MAX_BLOCKS - 1)) < b_own
    for p in range(N_PAIRS):
        g2 = gate[:, p * LANES:(p + 1) * LANES]
        out = jnp.full((t, IDS_LANES), -1.0, F32)
        for hh in range(2):
            half = (lane < MAX_BLOCKS) if hh == 0 else (lane >= MAX_BLOCKS)
            g = jnp.where(eligible & half, g2, -jnp.inf)
            ids = _top3_ids(g, lanef, float(hh * MAX_BLOCKS))
            for j in range(MOBA_TOPK):
                out = jnp.where(out_lane == 4 * hh + j, ids[j], out)
        ids_ref[p] = out.astype(I32)


def _moba_gate_prompt(q_f32, kmean):
    s = q_f32.shape[0]
    nb = kmean.shape[0]
    assert nb <= MAX_BLOCKS and s == nb * MOBA_BLOCK
    km = kmean.reshape(nb, MOBA_HEADS, MOBA_HEAD_DIM).transpose(1, 2, 0)
    km = jnp.pad(km, ((0, 0), (0, 0), (0, MAX_BLOCKS - nb)))
    eye = jnp.eye(MOBA_HEADS, dtype=F32)
    kmd = (km[:, :, None, :] * eye[:, None, :, None]).reshape(MOBA_WIDTH, MOBA_HEADS * MAX_BLOCKS)
    tm = MOBA_BLOCK
    return pl.pallas_call(
        _gate_kernel,
        grid=(s // tm,),
        in_specs=[pl.BlockSpec((tm, MOBA_WIDTH), lambda i: (i, 0)),
                  pl.BlockSpec(kmd.shape, lambda i: (0, 0))],
        out_specs=pl.BlockSpec((N_PAIRS, tm, IDS_LANES), lambda i: (0, i, 0)),
        out_shape=jax.ShapeDtypeStruct((N_PAIRS, s, IDS_LANES), I32),
        compiler_params=_cparams(("arbitrary",)),
        name="moba_gate_prompt",
    )(q_f32, kmd)


ATT_TILE = MOBA_BLOCK


def _flash_kernel(*refs, moba):
    if moba:
        slopes_ref, q_ref, k_ref, v_ref, ids_ref, o_ref, m_sc, l_sc, acc_sc, bias_sc = refs
    else:
        q_ref, k_ref, v_ref, o_ref, m_sc, l_sc, acc_sc = refs
    p_idx = pl.program_id(0)
    i = pl.program_id(1)
    tq = tk = ATT_TILE
    row = lax.broadcasted_iota(I32, (tq, tk), 0)
    col = lax.broadcasted_iota(I32, (tq, tk), 1)
    lane = lax.broadcasted_iota(I32, (tq, LANES), 1)
    low = lane < (LANES // 2)
    m_sc[...] = jnp.full(m_sc.shape, NEG_BIG, F32)
    l_sc[...] = jnp.zeros(l_sc.shape, F32)
    acc_sc[...] = jnp.zeros(acc_sc.shape, F32)
    q = q_ref[...]
    if moba:
        rel = (col - row).astype(F32)
        slope = [slopes_ref[2 * p_idx], slopes_ref[2 * p_idx + 1]]
        for hh in range(2):
            bias_sc[hh] = slope[hh] * rel
        qh = [jnp.where(low, q, jnp.zeros_like(q)), jnp.where(low, jnp.zeros_like(q), q)]
        ids = ids_ref[0]
    else:
        qh = [q[:, :HEAD_PAD], q[:, HEAD_PAD:]]

    def step(n, diag):
        start = pl.multiple_of(n * tk, tk)
        k_t = k_ref[pl.ds(start, tk), :]
        v_t = v_ref[pl.ds(start, tk), :]
        for hh in range(2):
            kk = k_t if moba else k_t[:, hh * HEAD_PAD:(hh + 1) * HEAD_PAD]
            s = _dot_nt(qh[hh], kk)
            if moba:
                s = s + bias_sc[hh]
            if diag:
                s = jnp.where(col <= row, s, -jnp.inf)
            mt = jnp.max(s, axis=1, keepdims=True)
            m_prev = m_sc[hh]
            if moba and not diag:
                c_n = -(slope[hh] * float(tk)) * _scalar_f32(i - n)
                sel = ids[:, 4 * hh:4 * hh + 1] == n
                for j in range(1, MOBA_TOPK):
                    sel = sel | (ids[:, 4 * hh + j:4 * hh + j + 1] == n)
                m_new = jnp.maximum(m_prev, jnp.where(sel, mt + c_n, NEG_BIG))
                shift = jnp.where(sel, m_new - c_n, jnp.inf)
            else:
                m_new = jnp.maximum(m_prev, mt)
                shift = m_new
            pr = jnp.exp(s - shift)
            alpha = jnp.exp(m_prev - m_new)
            l_sc[hh] = alpha * l_sc[hh] + jnp.sum(pr, axis=1, keepdims=True)
            acc_sc[hh] = alpha * acc_sc[hh] + _dot(pr.astype(BF16), v_t)
            m_sc[hh] = m_new

    def body(n, carry):
        step(n, False)
        return carry

    lax.fori_loop(0, i, body, 0)
    step(i, True)
    o0 = acc_sc[0] / l_sc[0]
    o1 = acc_sc[1] / l_sc[1]
    o_ref[...] = jnp.where(low, o0, o1).astype(o_ref.dtype)


def _alibi_slopes():
    return jnp.exp2(-8.0 * jnp.arange(1, MOBA_HEADS + 1, dtype=F32) / MOBA_HEADS)


def _flash_prompt(q, k, v, ids=None):
    moba = ids is not None
    s = q.shape[0]
    tq = ATT_TILE
    nq = s // tq
    qw = q.shape[1] // N_PAIRS
    scratch = [pltpu.VMEM((2, tq, 1), F32), pltpu.VMEM((2, tq, 1), F32), pltpu.VMEM((2, tq, LANES), F32)]
    if moba:
        in_specs = [pl.BlockSpec((tq, qw), lambda p, i, sl: (i, p)),
                    pl.BlockSpec((s, qw), lambda p, i, sl: (0, p)),
                    pl.BlockSpec((s, LANES), lambda p, i, sl: (0, p)),
                    pl.BlockSpec((1, tq, IDS_LANES), lambda p, i, sl: (p, i, 0))]
        out_spec = pl.BlockSpec((tq, LANES), lambda p, i, sl: (i, p))
        scratch.append(pltpu.VMEM((2, tq, tq), F32))
        nsp = 1
        args = (_alibi_slopes(), q, k, v, ids)
    else:
        in_specs = [pl.BlockSpec((tq, qw), lambda p, i: (i, p)),
                    pl.BlockSpec((s, qw), lambda p, i: (0, p)),
                    pl.BlockSpec((s, LANES), lambda p, i: (0, p))]
        out_spec = pl.BlockSpec((tq, LANES), lambda p, i: (i, p))
        nsp = 0
        args = (q, k, v)
    return pl.pallas_call(
        functools.partial(_flash_kernel, moba=moba),
        grid_spec=pltpu.PrefetchScalarGridSpec(
            num_scalar_prefetch=nsp, grid=(N_PAIRS, nq), in_specs=in_specs, out_specs=out_spec,
            scratch_shapes=scratch),
        out_shape=jax.ShapeDtypeStruct((s, N_PAIRS * LANES), BF16),
        compiler_params=_cparams(("arbitrary", "arbitrary")),
        name="moba_prompt" if moba else "mla_prompt",
    )(*args)


GATE_PAGES = 16
Q_ROWS = 32
PAGES_PER_BLOCK = MOBA_BLOCK // LANES


def _k_chunk_copies(pt_ref, k_hbm, buf, sem, g, slot, layer_off):
    base = g * GATE_PAGES
    return [pltpu.make_async_copy(k_hbm.at[pt_ref[base + j] + layer_off], buf.at[slot, j], sem.at[slot])
            for j in range(GATE_PAGES)]


def _moba_dec_gate_kernel(pt_ref, qblk_ref, k_hbm, ids_ref, buf, sem, ks_sc, *, n_pages, n_batch, layer_off):
    b = pl.program_id(0)
    n_chunks = n_pages // GATE_PAGES
    total = n_batch * n_chunks
    blocks_per_chunk = GATE_PAGES // PAGES_PER_BLOCK
    lane = lax.broadcasted_iota(I32, (MOBA_WIDTH, LANES), 1)

    @pl.when(b == 0)
    def _():
        for cp in _k_chunk_copies(pt_ref, k_hbm, buf, sem, 0, 0, layer_off):
            cp.start()

    ks_sc[...] = jnp.zeros(ks_sc.shape, F32)

    def chunk(c, carry):
        g = b * n_chunks + c
        slot = g % 2
        for cp in _k_chunk_copies(pt_ref, k_hbm, buf, sem, g, slot, layer_off):
            cp.wait()

        @pl.when(g + 1 < total)
        def _():
            for cp in _k_chunk_copies(pt_ref, k_hbm, buf, sem, g + 1, 1 - slot, layer_off):
                cp.start()

        for jb in range(blocks_per_chunk):
            blk = buf[slot, jb * PAGES_PER_BLOCK]
            for pg in range(1, PAGES_PER_BLOCK):
                blk = blk + buf[slot, jb * PAGES_PER_BLOCK + pg]
            ksum = jnp.sum(blk, axis=1, keepdims=True)
            n = c * blocks_per_chunk + jb
            ks_sc[...] = jnp.where(lane == n, ksum, ks_sc[...])
        return carry

    lax.fori_loop(0, n_chunks, chunk, 0)
    kmean = ks_sc[...] * (1.0 / MOBA_BLOCK)
    gate = _dot3(qblk_ref[0], kmean)
    n_full = n_pages // PAGES_PER_BLOCK
    glane = lax.broadcasted_iota(I32, (Q_ROWS, LANES), 1)
    g = jnp.where(glane < n_full, gate, -jnp.inf)
    ids = _top3_ids(g, glane.astype(F32), 0.0)
    out_lane = lax.broadcasted_iota(I32, (Q_ROWS, IDS_LANES), 1)
    out = jnp.full((Q_ROWS, IDS_LANES), -1.0, F32)
    for j in range(MOBA_TOPK):
        out = jnp.where(out_lane == j, ids[j], out)
    ids_ref[0] = out.astype(I32)


def _q_blockdiag(q):
    db, t, _ = q.shape
    eye = jnp.eye(MOBA_HEADS, dtype=q.dtype)
    q4 = q.reshape(db, t, 1, MOBA_HEADS, MOBA_HEAD_DIM) * eye[None, None, :, :, None]
    return q4.reshape(db, t * MOBA_HEADS, MOBA_WIDTH)


def _moba_dec_gate(q, kT_pages, page_table, layer_off):
    db, t, _ = q.shape
    n_pages = page_table.shape[1]
    assert t * MOBA_HEADS == Q_ROWS and n_pages % GATE_PAGES == 0
    assert n_pages % PAGES_PER_BLOCK == 0 and MOBA_TOPK <= n_pages // PAGES_PER_BLOCK <= LANES
    kern = functools.partial(_moba_dec_gate_kernel, n_pages=n_pages, n_batch=db, layer_off=layer_off)
    return pl.pallas_call(
        kern,
        grid_spec=pltpu.PrefetchScalarGridSpec(
            num_scalar_prefetch=1, grid=(db,),
            in_specs=[pl.BlockSpec((1, Q_ROWS, MOBA_WIDTH), lambda b, pt: (b, 0, 0)),
                      pl.BlockSpec(memory_space=pl.ANY)],
            out_specs=pl.BlockSpec((1, Q_ROWS, IDS_LANES), lambda b, pt: (b, 0, 0)),
            scratch_shapes=[pltpu.VMEM((2, GATE_PAGES, MOBA_WIDTH, LANES), F32),
                            pltpu.SemaphoreType.DMA((2,)),
                            pltpu.VMEM((MOBA_WIDTH, LANES), F32)]),
        out_shape=jax.ShapeDtypeStruct((db, Q_ROWS, IDS_LANES), I32),
        compiler_params=_cparams(("arbitrary",)),
        name="moba_sample_gate",
    )(page_table.reshape(-1), _q_blockdiag(q), kT_pages)


SEL_PAGES = MOBA_TOPK * PAGES_PER_BLOCK


def _sel_copies(pt_ref, ids_ref, k_hbm, v_hbm, kbuf, vbuf, sem, b, slot, r, n_pages, layer_off):
    h = r % MOBA_HEADS
    cps = []
    for j in range(MOBA_TOPK):
        n = ids_ref[(b * Q_ROWS + r) * MOBA_TOPK + j]
        for pg in range(PAGES_PER_BLOCK):
            page = pt_ref[b * n_pages + n * PAGES_PER_BLOCK + pg] + layer_off
            dst = r * SEL_PAGES + j * PAGES_PER_BLOCK + pg
            cps.append(pltpu.make_async_copy(k_hbm.at[page, h], kbuf.at[slot, dst], sem.at[slot]))
            cps.append(pltpu.make_async_copy(v_hbm.at[page, h], vbuf.at[slot, dst], sem.at[slot]))
    return cps


def _moba_dec_attn_kernel(pt_ref, ids_ref, slopes_ref, qb_ref, kn_ref, vn_ref, k_hbm, v_hbm, o_ref,
                          kbuf, vbuf, sem, *, n_pages, n_batch, layer_off):
    b = pl.program_id(0)
    slot = b % 2
    past = n_pages * LANES

    def issue(bb, sl):
        def body(r, carry):
            for cp in _sel_copies(pt_ref, ids_ref, k_hbm, v_hbm, kbuf, vbuf, sem, bb, sl, r, n_pages, layer_off):
                cp.start()
            return carry
        lax.fori_loop(0, Q_ROWS, body, 0)

    @pl.when(b == 0)
    def _():
        issue(0, 0)

    @pl.when(b + 1 < n_batch)
    def _():
        issue(b + 1, 1 - slot)

    def wait_body(r, carry):
        for cp in _sel_copies(pt_ref, ids_ref, k_hbm, v_hbm, kbuf, vbuf, sem, b, slot, r, n_pages, layer_off):
            cp.wait()
        return carry
    lax.fori_loop(0, Q_ROWS, wait_body, 0)

    lane = lax.broadcasted_iota(I32, (1, LANES), 1)
    lanef = lane.astype(F32)
    out_lane = lax.broadcasted_iota(I32, (MOBA_HEAD_DIM, LANES), 1)

    def row(r, acc):
        t = r // MOBA_HEADS
        h = r % MOBA_HEADS
        slope = slopes_ref[h]
        qb = qb_ref[0, r]
        qpos = _scalar_f32(past + t)
        scores = []
        for j in range(MOBA_TOPK):
            n = ids_ref[(b * Q_ROWS + r) * MOBA_TOPK + j]
            for pg in range(PAGES_PER_BLOCK):
                kt = kbuf[slot, r * SEL_PAGES + j * PAGES_PER_BLOCK + pg]
                s = jnp.sum(qb * kt, axis=0, keepdims=True) * MOBA_SCALE
                kpos = _scalar_f32(n * MOBA_BLOCK + pg * LANES) + lanef
                scores.append(s - slope * (qpos - kpos))
        hrow = pl.multiple_of(h * MOBA_HEAD_DIM, MOBA_HEAD_DIM)
        kt_own = kn_ref[0, pl.ds(hrow, MOBA_HEAD_DIM), :]
        s_own = jnp.sum(qb * kt_own, axis=0, keepdims=True) * MOBA_SCALE
        s_own = s_own - slope * (_scalar_f32(t) - lanef)
        scores.append(jnp.where(lane <= t, s_own, -jnp.inf))
        m = scores[0]
        for s in scores[1:]:
            m = jnp.maximum(m, s)
        m = jnp.max(m, axis=1, keepdims=True)
        ps = [jnp.exp(s - m) for s in scores]
        l = ps[0]
        for p_ in ps[1:]:
            l = l + p_
        l = jnp.sum(l, axis=1, keepdims=True)
        o = jnp.zeros((MOBA_HEAD_DIM, 1), F32)
        for idx in range(SEL_PAGES):
            vt = vbuf[slot, r * SEL_PAGES + idx]
            o = o + jnp.sum(vt * ps[idx], axis=1, keepdims=True)
        vt_own = vn_ref[0, pl.ds(hrow, MOBA_HEAD_DIM), :]
        o = o + jnp.sum(vt_own * ps[SEL_PAGES], axis=1, keepdims=True)
        o = o / l
        return jnp.where(out_lane == r, o, acc)

    acc = lax.fori_loop(0, Q_ROWS, row, jnp.zeros((MOBA_HEAD_DIM, LANES), F32))
    o_ref[0] = jnp.transpose(acc)[:Q_ROWS, :]


def _moba_dec_attn(q, k_new, v_new, kT_heads, vT_heads, page_table, ids, layer_off):
    db, t, _ = q.shape
    n_pages = page_table.shape[1]
    qrows = q.reshape(db, Q_ROWS, MOBA_HEAD_DIM)
    qb = jnp.broadcast_to(qrows[..., None], (db, Q_ROWS, MOBA_HEAD_DIM, LANES))
    pad_t = lambda a: jnp.pad(a.transpose(0, 2, 1), ((0, 0), (0, 0), (0, LANES - t)))
    ids_flat = ids[:, :, :MOBA_TOPK].reshape(-1)
    kern = functools.partial(_moba_dec_attn_kernel, n_pages=n_pages, n_batch=db, layer_off=layer_off)
    nbuf = Q_ROWS * SEL_PAGES
    out = pl.pallas_call(
        kern,
        grid_spec=pltpu.PrefetchScalarGridSpec(
            num_scalar_prefetch=3, grid=(db,),
            in_specs=[pl.BlockSpec((1, Q_ROWS, MOBA_HEAD_DIM, LANES), lambda b, *_: (b, 0, 0, 0)),
                      pl.BlockSpec((1, MOBA_WIDTH, LANES), lambda b, *_: (b, 0, 0)),
                      pl.BlockSpec((1, MOBA_WIDTH, LANES), lambda b, *_: (b, 0, 0)),
                      pl.BlockSpec(memory_space=pl.ANY), pl.BlockSpec(memory_space=pl.ANY)],
            out_specs=pl.BlockSpec((1, Q_ROWS, MOBA_HEAD_DIM), lambda b, *_: (b, 0, 0)),
            scratch_shapes=[pltpu.VMEM((2, nbuf, MOBA_HEAD_DIM, LANES), F32),
                            pltpu.VMEM((2, nbuf, MOBA_HEAD_DIM, LANES), F32),
                            pltpu.SemaphoreType.DMA((2,))]),
        out_shape=jax.ShapeDtypeStruct((db, Q_ROWS, MOBA_HEAD_DIM), F32),
        compiler_params=_cparams(("arbitrary",)),
        name="moba_sample_attn",
    )(page_table.reshape(-1), ids_flat, _alibi_slopes(), qb, pad_t(k_new), pad_t(v_new), kT_heads, vT_heads)
    return out.reshape(db, t, MOBA_WIDTH)


SUBLANES = 8
MLA_ROWS = MLA_HEADS * SUBLANES


def _mla_page_copies(pt_ref, ckv_hbm, kr_hbm, ckv_buf, kr_buf, sem, b, slot, p, n_pages, layer_off):
    page = pt_ref[b * n_pages + p] + layer_off
    off = pl.multiple_of(p * LANES, LANES)
    return [pltpu.make_async_copy(ckv_hbm.at[page], ckv_buf.at[slot, pl.ds(off, LANES), :], sem.at[slot]),
            pltpu.make_async_copy(kr_hbm.at[page], kr_buf.at[slot, :, pl.ds(off, LANES)], sem.at[slot])]


def _mla_dec_kernel(pt_ref, qs_ref, qr_ref, cn_ref, krn_ref, wuk_ref, wuv_ref, ckv_hbm, kr_hbm, o_ref,
                    ckv_buf, kr_buf, sem, *, n_pages, n_batch, layer_off):
    b = pl.program_id(0)
    slot = b % 2

    def issue(bb, sl):
        def body(p, carry):
            for cp in _mla_page_copies(pt_ref, ckv_hbm, kr_hbm, ckv_buf, kr_buf, sem, bb, sl, p, n_pages, layer_off):
                cp.start()
            return carry
        lax.fori_loop(0, n_pages, body, 0)

    @pl.when(b == 0)
    def _():
        issue(0, 0)

    @pl.when(b + 1 < n_batch)
    def _():
        issue(b + 1, 1 - slot)

    def wait_body(p, carry):
        for cp in _mla_page_copies(pt_ref, ckv_hbm, kr_hbm, ckv_buf, kr_buf, sem, b, slot, p, n_pages, layer_off):
            cp.wait()
        return carry
    lax.fori_loop(0, n_pages, wait_body, 0)

    qs = qs_ref[0]
    qr = qr_ref[0]
    q_lat = jnp.concatenate(
        [_dot(qs[h * SUBLANES:(h + 1) * SUBLANES], wuk_ref[h]) for h in range(MLA_HEADS)], axis=0).astype(BF16)
    ckv = ckv_buf[slot].astype(BF16)
    krt = kr_buf[slot].astype(BF16)
    cn = cn_ref[0].astype(BF16)
    krn = krn_ref[0].astype(BF16)
    s = _dot_nt(q_lat, ckv) + _dot(qr, krt)
    s_new = _dot_nt(q_lat, cn) + _dot_nt(qr, krn)
    t_row = lax.broadcasted_iota(I32, (MLA_ROWS, SUBLANES), 0) % SUBLANES
    t_col = lax.broadcasted_iota(I32, (MLA_ROWS, SUBLANES), 1)
    s_new = jnp.where(t_col <= t_row, s_new, -jnp.inf)
    m = jnp.maximum(jnp.max(s, axis=1, keepdims=True), jnp.max(s_new, axis=1, keepdims=True))
    p = jnp.exp(s - m)
    p_new = jnp.exp(s_new - m)
    l = jnp.sum(p, axis=1, keepdims=True) + jnp.sum(p_new, axis=1, keepdims=True)
    o_lat = (_dot(p.astype(BF16), ckv) + _dot(p_new.astype(BF16), cn)) / l
    o_lat = o_lat.astype(BF16)
    out = _dot(o_lat[:SUBLANES], wuv_ref[0])
    for h in range(1, MLA_HEADS):
        out = out + _dot(o_lat[h * SUBLANES:(h + 1) * SUBLANES], wuv_ref[h])
    o_ref[0] = out


def _mla_dec(q_mla, ckv_new, kr_new, ckv_pages, krT_pages, page_table, w_uk, w_uv, layer_off):
    db, t, _ = q_mla.shape
    n_pages = page_table.shape[1]
    past = n_pages * LANES
    assert t <= SUBLANES
    pad_rows = lambda a: jnp.pad(a, ((0, 0), (0, SUBLANES - t), (0, 0)))
    qs = pad_rows(q_mla).reshape(db, SUBLANES, MLA_HEADS, HEAD_PAD).transpose(0, 2, 1, 3).reshape(db, MLA_ROWS, HEAD_PAD)
    qr = qs[:, :, MLA_NOPE_DIM:MLA_QK_DIM]
    zk = jnp.zeros((MLA_HEADS, HEAD_PAD - MLA_NOPE_DIM, MLA_KV_RANK), F32)
    wuk_t = jnp.concatenate([w_uk.transpose(1, 2, 0), zk], axis=1).astype(BF16)
    eye = jnp.eye(MLA_HEADS, dtype=F32)
    wuv_e = (w_uv.transpose(1, 0, 2)[:, :, None, :] * eye[:, None, :, None]).reshape(MLA_HEADS, MLA_KV_RANK, MLA_WIDTH)
    wuv_e = wuv_e.astype(BF16)
    kern = functools.partial(_mla_dec_kernel, n_pages=n_pages, n_batch=db, layer_off=layer_off)
    per_b = lambda shp: pl.BlockSpec((1,) + shp, lambda b, pt: (b, 0, 0))
    out = pl.pallas_call(
        kern,
        grid_spec=pltpu.PrefetchScalarGridSpec(
            num_scalar_prefetch=1, grid=(db,),
            in_specs=[per_b((MLA_ROWS, HEAD_PAD)), per_b((MLA_ROWS, MLA_ROPE_DIM)),
                      per_b((SUBLANES, MLA_KV_RANK)), per_b((SUBLANES, MLA_ROPE_DIM)),
                      pl.BlockSpec(wuk_t.shape, lambda b, pt: (0, 0, 0)),
                      pl.BlockSpec(wuv_e.shape, lambda b, pt: (0, 0, 0)),
                      pl.BlockSpec(memory_space=pl.ANY), pl.BlockSpec(memory_space=pl.ANY)],
            out_specs=per_b((SUBLANES, MLA_WIDTH)),
            scratch_shapes=[pltpu.VMEM((2, past, MLA_KV_RANK), F32),
                            pltpu.VMEM((2, MLA_ROPE_DIM, past), F32),
                            pltpu.SemaphoreType.DMA((2,))]),
        out_shape=jax.ShapeDtypeStruct((db, SUBLANES, MLA_WIDTH), F32),
        compiler_params=_cparams(("arbitrary",)),
        name="mla_sample",
    )(page_table.reshape(-1), qs, qr, pad_rows(ckv_new), pad_rows(kr_new), wuk_t, wuv_e, ckv_pages, krT_pages)
    return out[:, :t, :]


GROUP_SIZE = N_EXPERTS // N_GROUPS


def _silu(x):
    return x * jax.nn.sigmoid(x)


def _route(sel, scores):
    t = sel.shape[0]
    lane = lax.broadcasted_iota(I32, (t, N_EXPERTS), 1)
    lanef = lane.astype(F32)
    grp = lane // GROUP_SIZE
    grpf = grp.astype(F32)
    gsc = jnp.zeros((t, N_EXPERTS), F32)
    for g in range(N_GROUPS):
        v = jnp.where(grp == g, sel, -jnp.inf)
        m1 = jnp.max(v, axis=1, keepdims=True)
        i1 = jnp.min(jnp.where(v == m1, lanef, 1e9), axis=1, keepdims=True)
        m2 = jnp.max(jnp.where(lanef == i1, -jnp.inf, v), axis=1, keepdims=True)
        gsc = jnp.where(grp == g, m1 + m2, gsc)
    emask = jnp.zeros((t, N_EXPERTS), jnp.bool_)
    for _ in range(TOPK_GROUPS):
        m = jnp.max(gsc, axis=1, keepdims=True)
        gi = jnp.min(jnp.where(gsc == m, grpf, 1e9), axis=1, keepdims=True)
        hit = grpf == gi
        emask = emask | hit
        gsc = jnp.where(hit, -jnp.inf, gsc)
    masked = jnp.where(emask, sel, -jnp.inf)
    chosen = jnp.zeros((t, N_EXPERTS), jnp.bool_)
    for _ in range(TOP_K):
        m = jnp.max(masked, axis=1, keepdims=True)
        ei = jnp.min(jnp.where(masked == m, lanef, 1e9), axis=1, keepdims=True)
        hit = lanef == ei
        chosen = chosen | hit
        masked = jnp.where(hit, -jnp.inf, masked)
    w = jnp.where(chosen, scores, 0.0)
    return w / jnp.sum(w, axis=1, keepdims=True) * ROUTED_SCALE, chosen


def _mid_kernel(x_ref, ma_ref, mb_ref, g1_ref, sh2_ref, sc2_ref, woa_ref, wob_ref, l1g_ref, l1b_ref,
                wrh_ref, wrl_ref, br_ref, wsg_ref, wsu_ref, wsd_ref, ltri_ref,
                x1_ref, h2_ref, shd_ref, wd_ref, rk_ref, cnt_ref, carry_sc, *, tiles_per_chunk):
    i = pl.program_id(0)
    a = _dot(ma_ref[...], woa_ref[...]) + _dot(mb_ref[...], wob_ref[...])
    y = DEEPNORM_ALPHA * x_ref[...] + g1_ref[...] * a
    x1 = _ln_plain(y) * l1g_ref[...] + l1b_ref[...]
    x1_ref[...] = x1
    h2 = _ln_plain(x1) * (1.0 + sc2_ref[...]) + sh2_ref[...]
    h2h, h2l = _split_bf16(h2)
    h2_ref[...] = h2h
    shd_ref[...] = _dot((_silu(_dot(h2h, wsg_ref[...])) * _dot(h2h, wsu_ref[...])).astype(BF16), wsd_ref[...])
    logits = _dot(h2h, wrh_ref[...]) + (_dot(h2h, wrl_ref[...]) + _dot(h2l, wrh_ref[...]))
    scores = jax.nn.sigmoid(logits)
    w, chosen = _route(scores + br_ref[...], scores)
    wd_ref[...] = w

    @pl.when(i % tiles_per_chunk == 0)
    def _():
        carry_sc[...] = jnp.zeros(carry_sc.shape, F32)

    onehot = jnp.where(chosen, 1.0, 0.0)
    before = _dot(ltri_ref[...], onehot.astype(BF16))
    carry = carry_sc[...]
    rk_ref[...] = jnp.where(chosen, carry + before, -1.0)
    carry = carry + jnp.sum(onehot, axis=0, keepdims=True)
    carry_sc[...] = carry
    cnt_ref[0] = carry.astype(I32)


def _mid_stage(x, mix_a, mix_b, g1, sh2, sc2, wts, chunk):
    woa, wob, l1g, l1b, wrh, wrl, br, wsg, wsu, wsd = wts
    m, d = x.shape
    tm = TOKEN_TILE
    nt = m // tm
    tpc = chunk // tm
    ltri = (jnp.arange(tm)[:, None] > jnp.arange(tm)[None, :]).astype(BF16)
    row = lambda i: (i, 0)
    fix = lambda i: (0, 0)
    mod_spec = pl.BlockSpec((tm, d), row) if g1.shape[0] == m else pl.BlockSpec((1, d), fix)
    full = lambda a: pl.BlockSpec(a.shape, fix)
    return pl.pallas_call(
        functools.partial(_mid_kernel, tiles_per_chunk=tpc),
        grid=(nt,),
        in_specs=[pl.BlockSpec((tm, d), row), pl.BlockSpec((tm, MOBA_WIDTH), row), pl.BlockSpec((tm, MLA_WIDTH), row),
                  mod_spec, mod_spec, mod_spec, full(woa), full(wob), full(l1g), full(l1b),
                  full(wrh), full(wrl), full(br), full(wsg), full(wsu), full(wsd), full(ltri)],
        out_specs=[pl.BlockSpec((tm, d), row), pl.BlockSpec((tm, d), row), pl.BlockSpec((tm, d), row),
                   pl.BlockSpec((tm, N_EXPERTS), row), pl.BlockSpec((tm, N_EXPERTS), row),
                   pl.BlockSpec((1, 1, N_EXPERTS), lambda i: (i // tpc, 0, 0))],
        out_shape=[jax.ShapeDtypeStruct((m, d), F32), jax.ShapeDtypeStruct((m, d), BF16),
                   jax.ShapeDtypeStruct((m, d), F32), jax.ShapeDtypeStruct((m, N_EXPERTS), F32),
                   jax.ShapeDtypeStruct((m, N_EXPERTS), F32), jax.ShapeDtypeStruct((m // chunk, 1, N_EXPERTS), I32)],
        scratch_shapes=[pltpu.VMEM((1, N_EXPERTS), F32)],
        compiler_params=_cparams(("arbitrary",)),
        name="mid_stage",
    )(x, mix_a, mix_b, g1, sh2, sc2, woa, wob, l1g, l1b, wrh, wrl, br, wsg, wsu, wsd, ltri)


MOE_ROWS = 128


def _moe_kernel(cnt_ref, rk_ref, wt_ref, h_ref, wg_ref, wu_ref, wd_ref, o_ref):
    c = pl.program_id(0)
    e = pl.program_id(1)
    chunk = h_ref.shape[0]

    @pl.when(e == 0)
    def _():
        o_ref[...] = jnp.zeros(o_ref.shape, F32)

    n = cnt_ref[c * N_EXPERTS + e]
    rk = rk_ref[0]
    wt = wt_ref[0]
    rowi = lax.broadcasted_iota(I32, (MOE_ROWS, chunk), 0).astype(F32)

    def sub(sb, carry):
        hit = rk == rowi + _scalar_f32(sb * MOE_ROWS)
        x = _dot(jnp.where(hit, 1.0, 0.0).astype(BF16), h_ref[...]).astype(BF16)
        act = (_silu(_dot(x, wg_ref[0])) * _dot(x, wu_ref[0])).astype(BF16)
        y = _dot(act, wd_ref[0]).astype(BF16)
        o_ref[...] += _dot_tn(jnp.where(hit, wt, 0.0).astype(BF16), y)
        return carry

    lax.fori_loop(0, (n + MOE_ROWS - 1) // MOE_ROWS, sub, 0)


def _moe_stage(h2, wd, rk, cnt, wg, wu, wdn, chunk):
    m, d = h2.shape
    nc = m // chunk
    rk_t = rk.T.reshape(N_EXPERTS, 1, m)
    wd_t = wd.T.reshape(N_EXPERTS, 1, m)
    return pl.pallas_call(
        _moe_kernel,
        grid_spec=pltpu.PrefetchScalarGridSpec(
            num_scalar_prefetch=1, grid=(nc, N_EXPERTS),
            in_specs=[pl.BlockSpec((1, 1, chunk), lambda c, e, cnt: (e, 0, c)),
                      pl.BlockSpec((1, 1, chunk), lambda c, e, cnt: (e, 0, c)),
                      pl.BlockSpec((chunk, d), lambda c, e, cnt: (c, 0)),
                      pl.BlockSpec((1, d, EXPERT_DIM), lambda c, e, cnt: (e, 0, 0)),
                      pl.BlockSpec((1, d, EXPERT_DIM), lambda c, e, cnt: (e, 0, 0)),
                      pl.BlockSpec((1, EXPERT_DIM, d), lambda c, e, cnt: (e, 0, 0))],
            out_specs=pl.BlockSpec((chunk, d), lambda c, e, cnt: (c, 0))),
        out_shape=jax.ShapeDtypeStruct((m, d), F32),
        compiler_params=_cparams(("arbitrary", "arbitrary")),
        name="moe_routed",
    )(cnt.reshape(-1), rk_t, wd_t, h2, wg, wu, wdn)


def _final_kernel(x1_ref, moe_ref, shd_ref, g2_ref, l2g_ref, l2b_ref, o_ref):
    y = DEEPNORM_ALPHA * x1_ref[...] + g2_ref[...] * (moe_ref[...] + shd_ref[...])
    o_ref[...] = _ln_plain(y) * l2g_ref[...] + l2b_ref[...]


def _final_stage(x1, moe, shd, g2, l2g, l2b):
    m, d = x1.shape
    tm = TOKEN_TILE
    row = lambda i: (i, 0)
    fix = lambda i: (0, 0)
    mod_spec = pl.BlockSpec((tm, d), row) if g2.shape[0] == m else pl.BlockSpec((1, d), fix)
    tile = pl.BlockSpec((tm, d), row)
    return pl.pallas_call(
        _final_kernel,
        grid=(m // tm,),
        in_specs=[tile, tile, tile, mod_spec, pl.BlockSpec((1, d), fix), pl.BlockSpec((1, d), fix)],
        out_specs=tile,
        out_shape=jax.ShapeDtypeStruct((m, d), F32),
        compiler_params=_cparams(("arbitrary",)),
        name="final_stage",
    )(x1, moe, shd, g2, l2g, l2b)


PROMPT_CHUNK = 1024


def _ffn_half(x, mix_a, mix_b, g1, sh2, sc2, g2, mid_wts, moe_wts, ln2, chunk):
    x1, h2, shd, wd, rk, cnt = _mid_stage(x, mix_a, mix_b, g1, sh2, sc2, mid_wts, chunk)
    moe = _moe_stage(h2, wd, rk, cnt, *moe_wts, chunk)
    return _final_stage(x1, moe, shd, g2, *ln2)


def kernel(x_prompt, x_sample, cache_moba_k, cache_moba_v, cache_mla_ckv, cache_mla_krope, page_table,
           c_prompt, c_sample, w_ada, b_ada, w_in, q_norm_g, kv_norm_g, w_uq, w_uk, w_uv, w_o,
           ln1_g, ln1_b, w_router, b_router, w_gate, w_up, w_down, ws_gate, ws_up, ws_down, ln2_g, ln2_b):
    assert w_ada.shape[0] == DEPTH == 1
    l = 0
    bp, s, d = x_prompt.shape
    db, t, _ = x_sample.shape
    assert bp == 1
    n_pool, page = cache_moba_k.shape[1], cache_moba_k.shape[2]
    assert page == LANES
    n_pages = page_table.shape[1]
    past = n_pages * page
    ms = db * t

    nb_c = bp + db
    c_all = jnp.pad(jnp.concatenate([c_prompt, c_sample], 0), ((0, -nb_c % SUBLANES), (0, 0)))
    ada = _adaln(c_all, w_ada[l], b_ada[l])
    mods_p = [ada[0:1, j * d:(j + 1) * d] for j in range(6)]
    mods_s = [jnp.repeat(ada[bp:nb_c, j * d:(j + 1) * d], t, axis=0) for j in range(6)]

    in_wts = _prep_in_weights(w_in[l], w_uq[l], w_uk[l], w_uv[l])
    row2 = lambda v: v.reshape(1, -1)
    wr_hi = w_router[l].astype(BF16)
    wr_lo = (w_router[l] - wr_hi.astype(F32)).astype(BF16)
    mid_wts = (w_o[l, :MOBA_WIDTH].astype(BF16), w_o[l, MOBA_WIDTH:].astype(BF16), row2(ln1_g[l]), row2(ln1_b[l]),
               wr_hi, wr_lo, row2(b_router[l]), ws_gate[l].astype(BF16), ws_up[l].astype(BF16), ws_down[l].astype(BF16))
    moe_wts = (w_gate[l].astype(BF16), w_up[l].astype(BF16), w_down[l].astype(BF16))
    ln2 = (row2(ln2_g[l]), row2(ln2_b[l]))

    xp = x_prompt.reshape(s, d)
    (qaf, qab, ka, va, kab, vab, ckv_p, kr_p, qm, km, vm, kmean) = _in_stage(
        xp, mods_p[0], mods_p[1], jnp.arange(s), in_wts, q_norm_g[l], kv_norm_g[l])
    ids = _moba_gate_prompt(qaf, kmean.reshape(-1, MOBA_WIDTH))
    mix_a = _flash_prompt(qab, kab, vab, ids)
    mix_b = _flash_prompt(qm, km, vm)
    y_prompt = _ffn_half(xp, mix_a, mix_b, mods_p[2], mods_p[3], mods_p[4], mods_p[5], mid_wts, moe_wts, ln2,
                         min(PROMPT_CHUNK, s))

    xs = x_sample.reshape(ms, d)
    pos_s = past + (jnp.arange(ms) % t)
    (qaf_s, _, ka_s, va_s, _, _, ckv_s, kr_s, qm_s, _, _, _) = _in_stage(
        xs, mods_s[0], mods_s[1], pos_s, in_wts, q_norm_g[l], kv_norm_g[l])
    kT = cache_moba_k.transpose(0, 1, 3, 4, 2).reshape(DEPTH * n_pool, MOBA_HEADS, MOBA_HEAD_DIM, page)
    vT = cache_moba_v.transpose(0, 1, 3, 4, 2).reshape(DEPTH * n_pool, MOBA_HEADS, MOBA_HEAD_DIM, page)
    krT = cache_mla_krope.transpose(0, 1, 3, 2).reshape(DEPTH * n_pool, MLA_ROPE_DIM, page)
    ckv_pages = cache_mla_ckv.reshape(DEPTH * n_pool, page, MLA_KV_RANK)
    layer_off = l * n_pool
    q3 = qaf_s.reshape(db, t, MOBA_WIDTH)
    ids_s = _moba_dec_gate(q3, kT.reshape(DEPTH * n_pool, MOBA_WIDTH, page), page_table, layer_off)
    mix_as = _moba_dec_attn(q3, ka_s.reshape(db, t, -1), va_s.reshape(db, t, -1), kT, vT, page_table, ids_s, layer_off)
    mix_bs = _mla_dec(qm_s.reshape(db, t, -1), ckv_s.reshape(db, t, -1), kr_s.reshape(db, t, -1),
                      ckv_pages, krT, page_table, w_uk[l], w_uv[l], layer_off)
    y_sample = _ffn_half(xs, mix_as.reshape(ms, -1).astype(BF16), mix_bs.reshape(ms, -1).astype(BF16),
                         mods_s[2], mods_s[3], mods_s[4], mods_s[5], mid_wts, moe_wts, ln2, ms)

    hd = (MOBA_HEADS, MOBA_HEAD_DIM)
    return (y_prompt.reshape(bp, s, d), y_sample.reshape(db, t, d),
            ka.reshape(DEPTH, bp, s, *hd), va.reshape(DEPTH, bp, s, *hd),
            ckv_p.reshape(DEPTH, bp, s, MLA_KV_RANK), kr_p.reshape(DEPTH, bp, s, MLA_ROPE_DIM),
            ka_s.reshape(DEPTH, db, t, *hd), va_s.reshape(DEPTH, db, t, *hd),
            ckv_s.reshape(DEPTH, db, t, MLA_KV_RANK), kr_s.reshape(DEPTH, db, t, MLA_ROPE_DIM))
```

```python
import functools
import math

import jax
import jax.numpy as jnp
from jax import lax
from jax.experimental import pallas as pl
from jax.experimental.pallas import tpu as pltpu

F32 = jnp.float32
BF16 = jnp.bfloat16
I32 = jnp.int32

D_MODEL = 1024
MOBA_HEADS = 8
MOBA_HEAD_DIM = 64
MOBA_BLOCK = 256
MOBA_TOPK = 3
MLA_HEADS = 8
MLA_Q_RANK = 256
MLA_KV_RANK = 128
MLA_NOPE_DIM = 64
MLA_ROPE_DIM = 32
MLA_V_DIM = 64
ROPE_THETA = 10000.0
MOBA_WIDTH = MOBA_HEADS * MOBA_HEAD_DIM
MLA_WIDTH = MLA_HEADS * MLA_V_DIM
N_EXPERTS = 64
EXPERT_DIM = 256
TOP_K = 8
N_GROUPS = 8
TOPK_GROUPS = 4
SHARED_DIM = 256
ROUTED_SCALE = 2.5
DEPTH = 1
DEEPNORM_ALPHA = (2 * DEPTH) ** 0.25
LN_EPS = 1e-5
RMS_EPS = 1e-6

LANES = 128
HEAD_PAD = 128
MLA_QK_DIM = MLA_NOPE_DIM + MLA_ROPE_DIM
MOBA_SCALE = MOBA_HEAD_DIM ** -0.5
MLA_SCALE = MLA_QK_DIM ** -0.5
LOG2E = math.log2(math.e)
MOBA_QSCALE = MOBA_SCALE * LOG2E
MLA_QSCALE = MLA_SCALE * LOG2E
NEG_BIG = -1e30
VMEM_LIMIT = 56 * 1024 * 1024

TOKEN_TILE = 256
_C_Q, _C_K, _C_V = 0, MOBA_WIDTH, 2 * MOBA_WIDTH
_C_CQ = 3 * MOBA_WIDTH
_C_CKV = _C_CQ + MLA_Q_RANK
_C_KR = _C_CKV + MLA_KV_RANK
_C_KRS = _C_KR + LANES
IN_COLS_EXT = _C_KRS + LANES


def _cparams(sem, vmem=VMEM_LIMIT):
    return pltpu.CompilerParams(dimension_semantics=sem, vmem_limit_bytes=vmem)


def _ln_plain(x):
    mu = jnp.mean(x, -1, keepdims=True)
    xc = x - mu
    var = jnp.mean(xc * xc, -1, keepdims=True)
    return xc * lax.rsqrt(var + LN_EPS)


def _dot(a, b):
    return jnp.dot(a, b, preferred_element_type=F32)


def _dot_nt(a, b):
    return lax.dot_general(a, b, (((1,), (1,)), ((), ())), preferred_element_type=F32)


def _dot_tn(a, b):
    return lax.dot_general(a, b, (((0,), (0,)), ((), ())), preferred_element_type=F32)


def _scalar_f32(v):
    return jnp.asarray(v, I32).astype(F32)


def _split_bf16(x):
    hi = x.astype(BF16)
    lo = (x - hi.astype(F32)).astype(BF16)
    return hi, lo


def _ada_kernel(c_ref, w_ref, b_ref, o_ref):
    c = c_ref[...]
    s = c * jax.nn.sigmoid(c)
    o_ref[...] = _dot(s.astype(BF16), w_ref[...].astype(BF16)) + b_ref[...]


def _adaln(c, w_ada, b_ada):
    b, d = c.shape
    n = w_ada.shape[1]
    tn = 1024
    return pl.pallas_call(
        _ada_kernel,
        grid=(n // tn,),
        in_specs=[pl.BlockSpec((b, d), lambda j: (0, 0)),
                  pl.BlockSpec((d, tn), lambda j: (0, j)),
                  pl.BlockSpec((1, tn), lambda j: (0, j))],
        out_specs=pl.BlockSpec((b, tn), lambda j: (0, j)),
        out_shape=jax.ShapeDtypeStruct((b, n), F32),
        compiler_params=_cparams(("arbitrary",)),
        name="adaln",
    )(c, w_ada, b_ada.reshape(1, n))


def _in_kernel(x_ref, sh_ref, sc_ref, win_ref, qg_ref, kvg_ref, wqa_ref, wqb_ref, wuk_ref, wuv_ref,
               ta_ref, tb_ref, tkc_ref, tks_ref,
               qaf_ref, qab_ref, ka_ref, va_ref, kab_ref, vab_ref, ckv_ref, kr_ref,
               qm_ref, km_ref, vm_ref, kmean_ref):
    x = x_ref[...]
    h = _ln_plain(x) * (1.0 + sc_ref[...]) + sh_ref[...]
    u = _dot(h.astype(BF16), win_ref[...])
    q_a = u[:, _C_Q:_C_K]
    k_a = u[:, _C_K:_C_V]
    v_a = u[:, _C_V:_C_CQ]
    qaf_ref[...] = q_a
    qab_ref[...] = (q_a * MOBA_QSCALE).astype(BF16)
    ka_ref[...] = k_a
    va_ref[...] = v_a
    kab_ref[...] = k_a.astype(BF16)
    vab_ref[...] = v_a.astype(BF16)
    kmean_ref[0] = jnp.mean(k_a, axis=0, keepdims=True)

    cq = u[:, _C_CQ:_C_CKV]
    cq = cq * lax.rsqrt(jnp.mean(cq * cq, -1, keepdims=True) + RMS_EPS) * qg_ref[...]
    cqb = cq.astype(BF16)
    qa = _dot(cqb, wqa_ref[...])
    qb = _dot(cqb, wqb_ref[...])
    ta = ta_ref[...]
    tb = tb_ref[...]
    for hh in range(MLA_HEADS):
        sl = slice(hh * HEAD_PAD, (hh + 1) * HEAD_PAD)
        qm_ref[:, sl] = (qa[:, sl] * ta + qb[:, sl] * tb).astype(BF16)

    ckv = u[:, _C_CKV:_C_KR]
    ckv = ckv * lax.rsqrt(jnp.mean(ckv * ckv, -1, keepdims=True) + RMS_EPS) * kvg_ref[...]
    ckv_ref[...] = ckv
    ckvb = ckv.astype(BF16)
    kr = u[:, _C_KR:_C_KRS] * tkc_ref[...] + u[:, _C_KRS:IN_COLS_EXT] * tks_ref[...]
    kr_ref[...] = pltpu.roll(kr, LANES - MLA_NOPE_DIM, 1)[:, :MLA_ROPE_DIM]
    kn = _dot(ckvb, wuk_ref[...])
    for hh in range(MLA_HEADS):
        sl = slice(hh * HEAD_PAD, (hh + 1) * HEAD_PAD)
        km_ref[:, sl] = (kn[:, sl] + kr).astype(BF16)
    vm_ref[...] = _dot(ckvb, wuv_ref[...]).astype(BF16)


def _rope_tables(pos):
    half = MLA_ROPE_DIM // 2
    inv = jnp.exp(-math.log(ROPE_THETA) * jnp.arange(half, dtype=F32) * (2.0 / MLA_ROPE_DIM))
    ang = pos.astype(F32)[:, None] * inv[None, :]
    cos = jnp.cos(ang)
    sin = jnp.sin(ang)
    cos2 = jnp.concatenate([cos, cos], -1)
    sin2 = jnp.concatenate([sin, sin], -1)
    t = pos.shape[0]
    z32 = jnp.zeros((t, MLA_ROPE_DIM), F32)
    z64 = jnp.zeros((t, MLA_NOPE_DIM), F32)
    ones = jnp.ones((t, MLA_NOPE_DIM), F32)
    ta = jnp.concatenate([ones, cos2, z32], -1) * MLA_QSCALE
    tb = jnp.concatenate([z64, sin2, z32], -1) * MLA_QSCALE
    tkc = jnp.concatenate([z64, cos2, z32], -1)
    tks = jnp.concatenate([z64, sin2, z32], -1)
    return ta, tb, tkc, tks


def _rot_half_cols(w):
    half = MLA_ROPE_DIM // 2
    return jnp.concatenate([-w[..., half:], w[..., :half]], -1)


def _prep_in_weights(w_in, w_uq, w_uk, w_uv):
    d = w_in.shape[0]
    o5 = 3 * MOBA_WIDTH + MLA_Q_RANK + MLA_KV_RANK
    w_kr = w_in[:, o5:]
    z64 = jnp.zeros((d, MLA_NOPE_DIM), F32)
    z32 = jnp.zeros((d, MLA_ROPE_DIM), F32)
    win_ext = jnp.concatenate([w_in[:, :o5], z64, w_kr, z32, z64, _rot_half_cols(w_kr), z32], -1).astype(BF16)
    wq = w_uq.reshape(MLA_Q_RANK, MLA_HEADS, MLA_QK_DIM)
    zq = jnp.zeros((MLA_Q_RANK, MLA_HEADS, MLA_ROPE_DIM), F32)
    zn = jnp.zeros((MLA_Q_RANK, MLA_HEADS, MLA_NOPE_DIM), F32)
    wqa = jnp.concatenate([wq, zq], -1).reshape(MLA_Q_RANK, MLA_HEADS * HEAD_PAD).astype(BF16)
    wqb = jnp.concatenate([zn, _rot_half_cols(wq[..., MLA_NOPE_DIM:]), zq], -1)
    wqb = wqb.reshape(MLA_Q_RANK, MLA_HEADS * HEAD_PAD).astype(BF16)
    zk = jnp.zeros((MLA_KV_RANK, MLA_HEADS, HEAD_PAD - MLA_NOPE_DIM), F32)
    wuk = jnp.concatenate([w_uk, zk], -1).reshape(MLA_KV_RANK, MLA_HEADS * HEAD_PAD).astype(BF16)
    wuv = w_uv.reshape(MLA_KV_RANK, MLA_WIDTH).astype(BF16)
    return win_ext, wqa, wqb, wuk, wuv


def _in_stage(x, sh, sc, pos, weights, q_norm_g, kv_norm_g):
    win_ext, wqa, wqb, wuk, wuv = weights
    m, d = x.shape
    tm = TOKEN_TILE
    nt = m // tm
    ta, tb, tkc, tks = _rope_tables(pos)
    row = lambda i: (i, 0)
    fix = lambda i: (0, 0)
    mod_spec = pl.BlockSpec((tm, d), row) if sh.shape[0] == m else pl.BlockSpec((1, d), fix)
    full = lambda a: pl.BlockSpec(a.shape, fix)
    hw = MLA_HEADS * HEAD_PAD
    outs = [
        ((m, MOBA_WIDTH), F32), ((m, MOBA_WIDTH), BF16),
        ((m, MOBA_WIDTH), F32), ((m, MOBA_WIDTH), F32),
        ((m, MOBA_WIDTH), BF16), ((m, MOBA_WIDTH), BF16),
        ((m, MLA_KV_RANK), F32), ((m, MLA_ROPE_DIM), F32),
        ((m, hw), BF16), ((m, hw), BF16), ((m, MLA_WIDTH), BF16),
    ]
    out_shape = [jax.ShapeDtypeStruct(s, t) for s, t in outs] + [jax.ShapeDtypeStruct((nt, 1, MOBA_WIDTH), F32)]
    out_specs = [pl.BlockSpec((tm, s[1]), row) for s, _ in outs] + [pl.BlockSpec((1, 1, MOBA_WIDTH), lambda i: (i, 0, 0))]
    return pl.pallas_call(
        _in_kernel,
        grid=(nt,),
        in_specs=[pl.BlockSpec((tm, d), row), mod_spec, mod_spec, full(win_ext),
                  pl.BlockSpec((1, MLA_Q_RANK), fix), pl.BlockSpec((1, MLA_KV_RANK), fix),
                  full(wqa), full(wqb), full(wuk), full(wuv),
                  pl.BlockSpec((tm, LANES), row), pl.BlockSpec((tm, LANES), row),
                  pl.BlockSpec((tm, LANES), row), pl.BlockSpec((tm, LANES), row)],
        out_specs=out_specs,
        out_shape=out_shape,
        compiler_params=_cparams(("arbitrary",)),
        name="in_stage",
    )(x, sh, sc, win_ext, q_norm_g.reshape(1, -1), kv_norm_g.reshape(1, -1), wqa, wqb, wuk, wuv, ta, tb, tkc, tks)


def _dot3(a, b):
    ah, al = _split_bf16(a)
    bh, bl = _split_bf16(b)
    return _dot(ah, bh) + (_dot(ah, bl) + _dot(al, bh))


MAX_BLOCKS = 64
N_PAIRS = MOBA_HEADS // 2
IDS_LANES = 8


def _top3_ids(g, lanef, base):
    ids = []
    for _ in range(MOBA_TOPK):
        m = jnp.max(g, axis=1, keepdims=True)
        idx = jnp.min(jnp.where(g == m, lanef, 1e9), axis=1, keepdims=True)
        ids.append(jnp.where(m > -jnp.inf, idx - base, -1.0))
        g = jnp.where(lanef == idx, -jnp.inf, g)
    return ids


def _gate_kernel(q_ref, km_ref, ids_ref):
    b_own = pl.program_id(0)
    gate = _dot3(q_ref[...], km_ref[...])
    t = gate.shape[0]
    lane = lax.broadcasted_iota(I32, (t, LANES), 1)
    lanef = lane.astype(F32)
    out_lane = lax.broadcasted_iota(I32, (t, IDS_LANES), 1)
    eligible = (lane & (MAX_BLOCKS - 1)) < b_own
    for p in range(N_PAIRS):
        g2 = gate[:, p * LANES:(p + 1) * LANES]
        out = jnp.full((t, IDS_LANES), -1.0, F32)
        for hh in range(2):
            half = (lane < MAX_BLOCKS) if hh == 0 else (lane >= MAX_BLOCKS)
            g = jnp.where(eligible & half, g2, -jnp.inf)
            ids = _top3_ids(g, lanef, float(hh * MAX_BLOCKS))
            for j in range(MOBA_TOPK):
                out = jnp.where(out_lane == 4 * hh + j, ids[j], out)
        ids_ref[p] = out.astype(I32)


def _moba_gate_prompt(q_f32, kmean):
    s = q_f32.shape[0]
    nb = kmean.shape[0]
    assert nb <= MAX_BLOCKS and s == nb * MOBA_BLOCK
    km = kmean.reshape(nb, MOBA_HEADS, MOBA_HEAD_DIM).transpose(1, 2, 0)
    km = jnp.pad(km, ((0, 0), (0, 0), (0, MAX_BLOCKS - nb)))
    eye = jnp.eye(MOBA_HEADS, dtype=F32)
    kmd = (km[:, :, None, :] * eye[:, None, :, None]).reshape(MOBA_WIDTH, MOBA_HEADS * MAX_BLOCKS)
    tm = MOBA_BLOCK
    return pl.pallas_call(
        _gate_kernel,
        grid=(s // tm,),
        in_specs=[pl.BlockSpec((tm, MOBA_WIDTH), lambda i: (i, 0)),
                  pl.BlockSpec(kmd.shape, lambda i: (0, 0))],
        out_specs=pl.BlockSpec((N_PAIRS, tm, IDS_LANES), lambda i: (0, i, 0)),
        out_shape=jax.ShapeDtypeStruct((N_PAIRS, s, IDS_LANES), I32),
        compiler_params=_cparams(("arbitrary",)),
        name="moba_gate_prompt",
    )(q_f32, kmd)


ATT_TILE = MOBA_BLOCK
ATT_GROUP = 4


def _flash_kernel(*refs, moba):
    if moba:
        slopes_ref, qt_ref, k_ref, vt_ref, ids_ref, o_ref = refs[:6]
        scr = refs[6:]
        m_sc, l_sc, acc_sc, bias_sc = scr[0:2], scr[2:4], scr[4:6], scr[6:8]
    else:
        qt_ref, k_ref, vt_ref, o_ref = refs[:4]
        scr = refs[4:]
        m_sc, l_sc, acc_sc = scr[0:2], scr[2:4], scr[4:6]
    p_idx = pl.program_id(0)
    i = pl.program_id(1)
    tq = tk = ATT_TILE
    hv = LANES // 2
    kidx = lax.broadcasted_iota(I32, (tk, tq), 0)
    qidx = lax.broadcasted_iota(I32, (tk, tq), 1)
    for hh in range(2):
        m_sc[hh][...] = jnp.full((1, tq), NEG_BIG, F32)
        l_sc[hh][...] = jnp.zeros((1, tq), F32)
        acc_sc[hh][...] = jnp.zeros((hv, tq), F32)
    qt = qt_ref[...]
    if moba:
        rel = (kidx - qidx).astype(F32)
        slope = [slopes_ref[2 * p_idx] * LOG2E, slopes_ref[2 * p_idx + 1] * LOG2E]
        for hh in range(2):
            bias_sc[hh][...] = slope[hh] * rel
        qrow = lax.broadcasted_iota(I32, qt.shape, 0)
        zero = jnp.zeros_like(qt)
        qh = [jnp.where(qrow < hv, qt, zero), jnp.where(qrow < hv, zero, qt)]
        ids = ids_ref[0]
    else:
        qh = [qt[:HEAD_PAD], qt[HEAD_PAD:]]

    def step(n0, nb, diag):
        start = pl.multiple_of(n0 * tk, tk)
        k_t = k_ref[pl.ds(start, nb * tk), :]
        v_t = vt_ref[:, pl.ds(start, nb * tk)]
        s_heads = [_dot(k_t if moba else k_t[:, hh * HEAD_PAD:(hh + 1) * HEAD_PAD], qh[hh]) for hh in range(2)]
        p_heads, alphas = [], []
        for hh in range(2):
            s_all = s_heads[hh]
            m_prev = m_sc[hh][...]
            m_new = m_prev
            blocks = []
            for b in range(nb):
                s = s_all[b * tk:(b + 1) * tk]
                if moba:
                    s = s + bias_sc[hh][...]
                if diag:
                    s = jnp.where(kidx <= qidx, s, -jnp.inf)
                mt = jnp.max(s, axis=0, keepdims=True)
                if moba and not diag:
                    n = n0 + b
                    c_n = -(slope[hh] * float(tk)) * _scalar_f32(i - n)
                    sel = ids[4 * hh:4 * hh + 1] == n
                    for j in range(1, MOBA_TOPK):
                        sel = sel | (ids[4 * hh + j:4 * hh + j + 1] == n)
                    m_new = jnp.maximum(m_new, jnp.where(sel, mt + c_n, NEG_BIG))
                    blocks.append((s, sel, c_n))
                else:
                    m_new = jnp.maximum(m_new, mt)
                    blocks.append((s, None, None))
            prs = []
            l_add = jnp.zeros((1, tq), F32)
            for s, sel, c_n in blocks:
                shift = m_new if sel is None else jnp.where(sel, m_new - c_n, jnp.inf)
                pr = jnp.exp2(s - shift)
                l_add = l_add + jnp.sum(pr, axis=0, keepdims=True)
                prs.append(pr.astype(BF16))
            p_heads.append(prs[0] if nb == 1 else jnp.concatenate(prs, axis=0))
            alpha = jnp.exp2(m_prev - m_new)
            alphas.append(alpha)
            l_sc[hh][...] = alpha * l_sc[hh][...] + l_add
            m_sc[hh][...] = m_new
        pv = [_dot(v_t[hh * hv:(hh + 1) * hv], p_heads[hh]) for hh in range(2)]
        for hh in range(2):
            acc_sc[hh][...] = alphas[hh] * acc_sc[hh][...] + pv[hh]

    def group_body(g, carry):
        step(g * ATT_GROUP, ATT_GROUP, False)
        return carry

    def single_body(r, carry):
        step((i // ATT_GROUP) * ATT_GROUP + r, 1, False)
        return carry

    lax.fori_loop(0, i // ATT_GROUP, group_body, 0)
    lax.fori_loop(0, i % ATT_GROUP, single_body, 0)
    step(i, 1, True)
    o_ref[:hv] = (acc_sc[0][...] / l_sc[0][...]).astype(o_ref.dtype)
    o_ref[hv:] = (acc_sc[1][...] / l_sc[1][...]).astype(o_ref.dtype)


def _alibi_slopes():
    return jnp.exp2(-8.0 * jnp.arange(1, MOBA_HEADS + 1, dtype=F32) / MOBA_HEADS)


def _flash_prompt(q, k, v, ids=None):
    moba = ids is not None
    s = q.shape[0]
    tq = ATT_TILE
    nq = s // tq
    qw = q.shape[1] // N_PAIRS
    qt = q.T
    vt = v.T
    scratch = ([pltpu.VMEM((1, tq), F32)] * 4 + [pltpu.VMEM((LANES // 2, tq), F32)] * 2)
    if moba:
        in_specs = [pl.BlockSpec((qw, tq), lambda p, i, sl: (p, i)),
                    pl.BlockSpec((s, qw), lambda p, i, sl: (0, p)),
                    pl.BlockSpec((LANES, s), lambda p, i, sl: (p, 0)),
                    pl.BlockSpec((1, IDS_LANES, tq), lambda p, i, sl: (p, 0, i))]
        out_spec = pl.BlockSpec((LANES, tq), lambda p, i, sl: (p, i))
        scratch = scratch + [pltpu.VMEM((tq, tq), F32)] * 2
        nsp = 1
        args = (_alibi_slopes(), qt, k, vt, ids.transpose(0, 2, 1))
    else:
        in_specs = [pl.BlockSpec((qw, tq), lambda p, i: (p, i)),
                    pl.BlockSpec((s, qw), lambda p, i: (0, p)),
                    pl.BlockSpec((LANES, s), lambda p, i: (p, 0))]
        out_spec = pl.BlockSpec((LANES, tq), lambda p, i: (p, i))
        nsp = 0
        args = (qt, k, vt)
    out_t = pl.pallas_call(
        functools.partial(_flash_kernel, moba=moba),
        grid_spec=pltpu.PrefetchScalarGridSpec(
            num_scalar_prefetch=nsp, grid=(N_PAIRS, nq), in_specs=in_specs, out_specs=out_spec,
            scratch_shapes=scratch),
        out_shape=jax.ShapeDtypeStruct((N_PAIRS * LANES, s), BF16),
        compiler_params=_cparams(("arbitrary", "arbitrary")),
        name="moba_prompt" if moba else "mla_prompt",
    )(*args)
    return out_t.T


GATE_PAGES = 8
GATE_SLOTS = 4
Q_ROWS = 32
PAGES_PER_BLOCK = MOBA_BLOCK // LANES


def _k_chunk_copies(pt_ref, k_hbm, buf, sem, g, slot, layer_off):
    base = g * GATE_PAGES
    return [pltpu.make_async_copy(k_hbm.at[pt_ref[base + j] + layer_off], buf.at[slot, j], sem.at[slot])
            for j in range(GATE_PAGES)]


def _moba_dec_gate_kernel(pt_ref, qblk_ref, k_hbm, ids_ref, buf, sem, ks_sc, *, n_pages, n_batch, layer_off):
    b = pl.program_id(0)
    n_chunks = n_pages // GATE_PAGES
    total = n_batch * n_chunks
    ahead = GATE_SLOTS - 1
    blocks_per_chunk = GATE_PAGES // PAGES_PER_BLOCK
    lane = lax.broadcasted_iota(I32, (MOBA_WIDTH, LANES), 1)

    @pl.when(b == 0)
    def _():
        for g0 in range(min(ahead, total)):
            for cp in _k_chunk_copies(pt_ref, k_hbm, buf, sem, g0, g0, layer_off):
                cp.start()

    ks_sc[...] = jnp.zeros(ks_sc.shape, F32)

    def chunk(c, carry):
        g = b * n_chunks + c
        slot = g % GATE_SLOTS
        for cp in _k_chunk_copies(pt_ref, k_hbm, buf, sem, g, slot, layer_off):
            cp.wait()

        @pl.when(g + ahead < total)
        def _():
            for cp in _k_chunk_copies(pt_ref, k_hbm, buf, sem, g + ahead, (g + ahead) % GATE_SLOTS, layer_off):
                cp.start()

        for jb in range(blocks_per_chunk):
            blk = buf[slot, jb * PAGES_PER_BLOCK]
            for pg in range(1, PAGES_PER_BLOCK):
                blk = blk + buf[slot, jb * PAGES_PER_BLOCK + pg]
            ksum = jnp.sum(blk, axis=1, keepdims=True)
            n = c * blocks_per_chunk + jb
            ks_sc[...] = jnp.where(lane == n, ksum, ks_sc[...])
        return carry

    lax.fori_loop(0, n_chunks, chunk, 0)
    kmean = ks_sc[...] * (1.0 / MOBA_BLOCK)
    gate = _dot3(qblk_ref[0], kmean)
    n_full = n_pages // PAGES_PER_BLOCK
    glane = lax.broadcasted_iota(I32, (Q_ROWS, LANES), 1)
    g = jnp.where(glane < n_full, gate, -jnp.inf)
    ids = _top3_ids(g, glane.astype(F32), 0.0)
    out_lane = lax.broadcasted_iota(I32, (Q_ROWS, IDS_LANES), 1)
    out = jnp.full((Q_ROWS, IDS_LANES), -1.0, F32)
    for j in range(MOBA_TOPK):
        out = jnp.where(out_lane == j, ids[j], out)
    ids_ref[0] = out.astype(I32)


def _q_blockdiag(q):
    db, t, _ = q.shape
    eye = jnp.eye(MOBA_HEADS, dtype=q.dtype)
    q4 = q.reshape(db, t, 1, MOBA_HEADS, MOBA_HEAD_DIM) * eye[None, None, :, :, None]
    return q4.reshape(db, t * MOBA_HEADS, MOBA_WIDTH)


def _moba_dec_gate(q, kT_pages, page_table, layer_off):
    db, t, _ = q.shape
    n_pages = page_table.shape[1]
    assert t * MOBA_HEADS == Q_ROWS and n_pages % GATE_PAGES == 0
    assert n_pages % PAGES_PER_BLOCK == 0 and MOBA_TOPK <= n_pages // PAGES_PER_BLOCK <= LANES
    kern = functools.partial(_moba_dec_gate_kernel, n_pages=n_pages, n_batch=db, layer_off=layer_off)
    return pl.pallas_call(
        kern,
        grid_spec=pltpu.PrefetchScalarGridSpec(
            num_scalar_prefetch=1, grid=(db,),
            in_specs=[pl.BlockSpec((1, Q_ROWS, MOBA_WIDTH), lambda b, pt: (b, 0, 0)),
                      pl.BlockSpec(memory_space=pl.ANY)],
            out_specs=pl.BlockSpec((1, Q_ROWS, IDS_LANES), lambda b, pt: (b, 0, 0)),
            scratch_shapes=[pltpu.VMEM((GATE_SLOTS, GATE_PAGES, MOBA_WIDTH, LANES), F32),
                            pltpu.SemaphoreType.DMA((GATE_SLOTS,)),
                            pltpu.VMEM((MOBA_WIDTH, LANES), F32)]),
        out_shape=jax.ShapeDtypeStruct((db, Q_ROWS, IDS_LANES), I32),
        compiler_params=_cparams(("arbitrary",)),
        name="moba_sample_gate",
    )(page_table.reshape(-1), _q_blockdiag(q), kT_pages)


SEL_PAGES = MOBA_TOPK * PAGES_PER_BLOCK


def _sel_copies(pt_ref, ids_ref, k_hbm, v_hbm, kbuf, vbuf, sem, b, slot, r, n_pages, layer_off):
    h = r % MOBA_HEADS
    cps = []
    for j in range(MOBA_TOPK):
        n = ids_ref[(b * Q_ROWS + r) * MOBA_TOPK + j]
        for pg in range(PAGES_PER_BLOCK):
            page = pt_ref[b * n_pages + n * PAGES_PER_BLOCK + pg] + layer_off
            dst = r * SEL_PAGES + j * PAGES_PER_BLOCK + pg
            cps.append(pltpu.make_async_copy(k_hbm.at[page, h], kbuf.at[slot, dst], sem.at[slot]))
            cps.append(pltpu.make_async_copy(v_hbm.at[page, h], vbuf.at[slot, dst], sem.at[slot]))
    return cps


def _moba_dec_attn_kernel(pt_ref, ids_ref, slopes_ref, qb_ref, kn_ref, vn_ref, k_hbm, v_hbm, o_ref,
                          kbuf, vbuf, sem, *, n_pages, n_batch, layer_off):
    b = pl.program_id(0)
    slot = b % 2
    past = n_pages * LANES

    def issue(bb, sl):
        def body(r, carry):
            for cp in _sel_copies(pt_ref, ids_ref, k_hbm, v_hbm, kbuf, vbuf, sem, bb, sl, r, n_pages, layer_off):
                cp.start()
            return carry
        lax.fori_loop(0, Q_ROWS, body, 0)

    @pl.when(b == 0)
    def _():
        issue(0, 0)

    @pl.when(b + 1 < n_batch)
    def _():
        issue(b + 1, 1 - slot)

    def wait_body(r, carry):
        for cp in _sel_copies(pt_ref, ids_ref, k_hbm, v_hbm, kbuf, vbuf, sem, b, slot, r, n_pages, layer_off):
            cp.wait()
        return carry
    lax.fori_loop(0, Q_ROWS, wait_body, 0)

    lane = lax.broadcasted_iota(I32, (1, LANES), 1)
    lanef = lane.astype(F32)
    out_lane = lax.broadcasted_iota(I32, (MOBA_HEAD_DIM, LANES), 1)

    def row(r, acc):
        t = r // MOBA_HEADS
        h = r % MOBA_HEADS
        slope = slopes_ref[h]
        qb = qb_ref[0, r]
        qpos = _scalar_f32(past + t)
        scores = []
        for j in range(MOBA_TOPK):
            n = ids_ref[(b * Q_ROWS + r) * MOBA_TOPK + j]
            for pg in range(PAGES_PER_BLOCK):
                kt = kbuf[slot, r * SEL_PAGES + j * PAGES_PER_BLOCK + pg]
                s = jnp.sum(qb * kt, axis=0, keepdims=True) * MOBA_SCALE
                kpos = _scalar_f32(n * MOBA_BLOCK + pg * LANES) + lanef
                scores.append(s - slope * (qpos - kpos))
        hrow = pl.multiple_of(h * MOBA_HEAD_DIM, MOBA_HEAD_DIM)
        kt_own = kn_ref[0, pl.ds(hrow, MOBA_HEAD_DIM), :]
        s_own = jnp.sum(qb * kt_own, axis=0, keepdims=True) * MOBA_SCALE
        s_own = s_own - slope * (_scalar_f32(t) - lanef)
        scores.append(jnp.where(lane <= t, s_own, -jnp.inf))
        m = scores[0]
        for s in scores[1:]:
            m = jnp.maximum(m, s)
        m = jnp.max(m, axis=1, keepdims=True)
        ps = [jnp.exp(s - m) for s in scores]
        l = ps[0]
        for p_ in ps[1:]:
            l = l + p_
        l = jnp.sum(l, axis=1, keepdims=True)
        o = jnp.zeros((MOBA_HEAD_DIM, 1), F32)
        for idx in range(SEL_PAGES):
            vt = vbuf[slot, r * SEL_PAGES + idx]
            o = o + jnp.sum(vt * ps[idx], axis=1, keepdims=True)
        vt_own = vn_ref[0, pl.ds(hrow, MOBA_HEAD_DIM), :]
        o = o + jnp.sum(vt_own * ps[SEL_PAGES], axis=1, keepdims=True)
        o = o / l
        return jnp.where(out_lane == r, o, acc)

    acc = lax.fori_loop(0, Q_ROWS, row, jnp.zeros((MOBA_HEAD_DIM, LANES), F32), unroll=4)
    o_ref[0] = jnp.transpose(acc)[:Q_ROWS, :]


def _moba_dec_attn(q, k_new, v_new, kT_heads, vT_heads, page_table, ids, layer_off):
    db, t, _ = q.shape
    n_pages = page_table.shape[1]
    qrows = q.reshape(db, Q_ROWS, MOBA_HEAD_DIM)
    qb = jnp.broadcast_to(qrows[..., None], (db, Q_ROWS, MOBA_HEAD_DIM, LANES))
    pad_t = lambda a: jnp.pad(a.transpose(0, 2, 1), ((0, 0), (0, 0), (0, LANES - t)))
    ids_flat = ids[:, :, :MOBA_TOPK].reshape(-1)
    kern = functools.partial(_moba_dec_attn_kernel, n_pages=n_pages, n_batch=db, layer_off=layer_off)
    nbuf = Q_ROWS * SEL_PAGES
    out = pl.pallas_call(
        kern,
        grid_spec=pltpu.PrefetchScalarGridSpec(
            num_scalar_prefetch=3, grid=(db,),
            in_specs=[pl.BlockSpec((1, Q_ROWS, MOBA_HEAD_DIM, LANES), lambda b, *_: (b, 0, 0, 0)),
                      pl.BlockSpec((1, MOBA_WIDTH, LANES), lambda b, *_: (b, 0, 0)),
                      pl.BlockSpec((1, MOBA_WIDTH, LANES), lambda b, *_: (b, 0, 0)),
                      pl.BlockSpec(memory_space=pl.ANY), pl.BlockSpec(memory_space=pl.ANY)],
            out_specs=pl.BlockSpec((1, Q_ROWS, MOBA_HEAD_DIM), lambda b, *_: (b, 0, 0)),
            scratch_shapes=[pltpu.VMEM((2, nbuf, MOBA_HEAD_DIM, LANES), F32),
                            pltpu.VMEM((2, nbuf, MOBA_HEAD_DIM, LANES), F32),
                            pltpu.SemaphoreType.DMA((2,))]),
        out_shape=jax.ShapeDtypeStruct((db, Q_ROWS, MOBA_HEAD_DIM), F32),
        compiler_params=_cparams(("arbitrary",)),
        name="moba_sample_attn",
    )(page_table.reshape(-1), ids_flat, _alibi_slopes(), qb, pad_t(k_new), pad_t(v_new), kT_heads, vT_heads)
    return out.reshape(db, t, MOBA_WIDTH)


SUBLANES = 8
MLA_ROWS = MLA_HEADS * SUBLANES


def _mla_page_copies(pt_ref, ckv_hbm, kr_hbm, ckv_buf, kr_buf, sem, b, slot, p, n_pages, layer_off):
    page = pt_ref[b * n_pages + p] + layer_off
    off = pl.multiple_of(p * LANES, LANES)
    return [pltpu.make_async_copy(ckv_hbm.at[page], ckv_buf.at[slot, pl.ds(off, LANES), :], sem.at[slot]),
            pltpu.make_async_copy(kr_hbm.at[page], kr_buf.at[slot, :, pl.ds(off, LANES)], sem.at[slot])]


def _mla_dec_kernel(pt_ref, qs_ref, qr_ref, cn_ref, krn_ref, wuk_ref, wuv_ref, ckv_hbm, kr_hbm, o_ref,
                    ckv_buf, kr_buf, sem, *, n_pages, n_batch, layer_off):
    b = pl.program_id(0)
    slot = b % 2

    def issue(bb, sl):
        def body(p, carry):
            for cp in _mla_page_copies(pt_ref, ckv_hbm, kr_hbm, ckv_buf, kr_buf, sem, bb, sl, p, n_pages, layer_off):
                cp.start()
            return carry
        lax.fori_loop(0, n_pages, body, 0)

    @pl.when(b == 0)
    def _():
        issue(0, 0)

    @pl.when(b + 1 < n_batch)
    def _():
        issue(b + 1, 1 - slot)

    def wait_body(p, carry):
        for cp in _mla_page_copies(pt_ref, ckv_hbm, kr_hbm, ckv_buf, kr_buf, sem, b, slot, p, n_pages, layer_off):
            cp.wait()
        return carry
    lax.fori_loop(0, n_pages, wait_body, 0)

    qs = qs_ref[0]
    qr = qr_ref[0]
    q_lat = jnp.concatenate(
        [_dot(qs[h * SUBLANES:(h + 1) * SUBLANES], wuk_ref[h]) for h in range(MLA_HEADS)], axis=0).astype(BF16)
    ckv = ckv_buf[slot].astype(BF16)
    krt = kr_buf[slot].astype(BF16)
    cn = cn_ref[0].astype(BF16)
    krn = krn_ref[0].astype(BF16)
    s = _dot_nt(q_lat, ckv) + _dot(qr, krt)
    s_new = _dot_nt(q_lat, cn) + _dot_nt(qr, krn)
    t_row = lax.broadcasted_iota(I32, (MLA_ROWS, SUBLANES), 0) % SUBLANES
    t_col = lax.broadcasted_iota(I32, (MLA_ROWS, SUBLANES), 1)
    s_new = jnp.where(t_col <= t_row, s_new, -jnp.inf)
    m = jnp.maximum(jnp.max(s, axis=1, keepdims=True), jnp.max(s_new, axis=1, keepdims=True))
    p = jnp.exp2(s - m)
    p_new = jnp.exp2(s_new - m)
    l = jnp.sum(p, axis=1, keepdims=True) + jnp.sum(p_new, axis=1, keepdims=True)
    o_lat = (_dot(p.astype(BF16), ckv) + _dot(p_new.astype(BF16), cn)) / l
    o_lat = o_lat.astype(BF16)
    out = _dot(o_lat[:SUBLANES], wuv_ref[0])
    for h in range(1, MLA_HEADS):
        out = out + _dot(o_lat[h * SUBLANES:(h + 1) * SUBLANES], wuv_ref[h])
    o_ref[0] = out


def _mla_dec(q_mla, ckv_new, kr_new, ckv_pages, krT_pages, page_table, w_uk, w_uv, layer_off):
    db, t, _ = q_mla.shape
    n_pages = page_table.shape[1]
    past = n_pages * LANES
    assert t <= SUBLANES
    pad_rows = lambda a: jnp.pad(a, ((0, 0), (0, SUBLANES - t), (0, 0)))
    qs = pad_rows(q_mla).reshape(db, SUBLANES, MLA_HEADS, HEAD_PAD).transpose(0, 2, 1, 3).reshape(db, MLA_ROWS, HEAD_PAD)
    qr = qs[:, :, MLA_NOPE_DIM:MLA_QK_DIM]
    zk = jnp.zeros((MLA_HEADS, HEAD_PAD - MLA_NOPE_DIM, MLA_KV_RANK), F32)
    wuk_t = jnp.concatenate([w_uk.transpose(1, 2, 0), zk], axis=1).astype(BF16)
    eye = jnp.eye(MLA_HEADS, dtype=F32)
    wuv_e = (w_uv.transpose(1, 0, 2)[:, :, None, :] * eye[:, None, :, None]).reshape(MLA_HEADS, MLA_KV_RANK, MLA_WIDTH)
    wuv_e = wuv_e.astype(BF16)
    kern = functools.partial(_mla_dec_kernel, n_pages=n_pages, n_batch=db, layer_off=layer_off)
    per_b = lambda shp: pl.BlockSpec((1,) + shp, lambda b, pt: (b, 0, 0))
    out = pl.pallas_call(
        kern,
        grid_spec=pltpu.PrefetchScalarGridSpec(
            num_scalar_prefetch=1, grid=(db,),
            in_specs=[per_b((MLA_ROWS, HEAD_PAD)), per_b((MLA_ROWS, MLA_ROPE_DIM)),
                      per_b((SUBLANES, MLA_KV_RANK)), per_b((SUBLANES, MLA_ROPE_DIM)),
                      pl.BlockSpec(wuk_t.shape, lambda b, pt: (0, 0, 0)),
                      pl.BlockSpec(wuv_e.shape, lambda b, pt: (0, 0, 0)),
                      pl.BlockSpec(memory_space=pl.ANY), pl.BlockSpec(memory_space=pl.ANY)],
            out_specs=per_b((SUBLANES, MLA_WIDTH)),
            scratch_shapes=[pltpu.VMEM((2, past, MLA_KV_RANK), F32),
                            pltpu.VMEM((2, MLA_ROPE_DIM, past), F32),
                            pltpu.SemaphoreType.DMA((2,))]),
        out_shape=jax.ShapeDtypeStruct((db, SUBLANES, MLA_WIDTH), F32),
        compiler_params=_cparams(("arbitrary",)),
        name="mla_sample",
    )(page_table.reshape(-1), qs, qr, pad_rows(ckv_new), pad_rows(kr_new), wuk_t, wuv_e, ckv_pages, krT_pages)
    return out[:, :t, :]


GROUP_SIZE = N_EXPERTS // N_GROUPS


def _silu(x):
    return x * jax.nn.sigmoid(x)


def _route(sel, scores):
    t = sel.shape[0]
    lane = lax.broadcasted_iota(I32, (t, N_EXPERTS), 1)
    lanef = lane.astype(F32)
    grp = lane // GROUP_SIZE
    grpf = grp.astype(F32)
    gsc = jnp.zeros((t, N_EXPERTS), F32)
    for g in range(N_GROUPS):
        v = jnp.where(grp == g, sel, -jnp.inf)
        m1 = jnp.max(v, axis=1, keepdims=True)
        i1 = jnp.min(jnp.where(v == m1, lanef, 1e9), axis=1, keepdims=True)
        m2 = jnp.max(jnp.where(lanef == i1, -jnp.inf, v), axis=1, keepdims=True)
        gsc = jnp.where(grp == g, m1 + m2, gsc)
    emask = jnp.zeros((t, N_EXPERTS), jnp.bool_)
    for _ in range(TOPK_GROUPS):
        m = jnp.max(gsc, axis=1, keepdims=True)
        gi = jnp.min(jnp.where(gsc == m, grpf, 1e9), axis=1, keepdims=True)
        hit = grpf == gi
        emask = emask | hit
        gsc = jnp.where(hit, -jnp.inf, gsc)
    masked = jnp.where(emask, sel, -jnp.inf)
    chosen = jnp.zeros((t, N_EXPERTS), jnp.bool_)
    for _ in range(TOP_K):
        m = jnp.max(masked, axis=1, keepdims=True)
        ei = jnp.min(jnp.where(masked == m, lanef, 1e9), axis=1, keepdims=True)
        hit = lanef == ei
        chosen = chosen | hit
        masked = jnp.where(hit, -jnp.inf, masked)
    w = jnp.where(chosen, scores, 0.0)
    return w / jnp.sum(w, axis=1, keepdims=True) * ROUTED_SCALE, chosen


def _mid_kernel(x_ref, ma_ref, mb_ref, g1_ref, sh2_ref, sc2_ref, woa_ref, wob_ref, l1g_ref, l1b_ref,
                wrh_ref, wrl_ref, br_ref, wsg_ref, wsu_ref, wsd_ref, ltri_ref,
                x1_ref, h2_ref, shd_ref, wd_ref, rk_ref, cnt_ref, carry_sc, *, tiles_per_chunk):
    i = pl.program_id(0)
    a = _dot(ma_ref[...], woa_ref[...]) + _dot(mb_ref[...], wob_ref[...])
    y = DEEPNORM_ALPHA * x_ref[...] + g1_ref[...] * a
    x1 = _ln_plain(y) * l1g_ref[...] + l1b_ref[...]
    x1_ref[...] = x1
    h2 = _ln_plain(x1) * (1.0 + sc2_ref[...]) + sh2_ref[...]
    h2h, h2l = _split_bf16(h2)
    h2_ref[...] = h2h
    shd_ref[...] = _dot((_silu(_dot(h2h, wsg_ref[...])) * _dot(h2h, wsu_ref[...])).astype(BF16), wsd_ref[...])
    logits = _dot(h2h, wrh_ref[...]) + (_dot(h2h, wrl_ref[...]) + _dot(h2l, wrh_ref[...]))
    scores = jax.nn.sigmoid(logits)
    w, chosen = _route(scores + br_ref[...], scores)
    wd_ref[...] = w

    @pl.when(i % tiles_per_chunk == 0)
    def _():
        carry_sc[...] = jnp.zeros(carry_sc.shape, F32)

    onehot = jnp.where(chosen, 1.0, 0.0)
    before = _dot(ltri_ref[...], onehot.astype(BF16))
    carry = carry_sc[...]
    rk_ref[...] = jnp.where(chosen, carry + before, -1.0)
    carry = carry + jnp.sum(onehot, axis=0, keepdims=True)
    carry_sc[...] = carry
    cnt_ref[0] = carry.astype(I32)


def _mid_stage(x, mix_a, mix_b, g1, sh2, sc2, wts, chunk):
    woa, wob, l1g, l1b, wrh, wrl, br, wsg, wsu, wsd = wts
    m, d = x.shape
    tm = TOKEN_TILE
    nt = m // tm
    tpc = chunk // tm
    ltri = (jnp.arange(tm)[:, None] > jnp.arange(tm)[None, :]).astype(BF16)
    row = lambda i: (i, 0)
    fix = lambda i: (0, 0)
    mod_spec = pl.BlockSpec((tm, d), row) if g1.shape[0] == m else pl.BlockSpec((1, d), fix)
    full = lambda a: pl.BlockSpec(a.shape, fix)
    return pl.pallas_call(
        functools.partial(_mid_kernel, tiles_per_chunk=tpc),
        grid=(nt,),
        in_specs=[pl.BlockSpec((tm, d), row), pl.BlockSpec((tm, MOBA_WIDTH), row), pl.BlockSpec((tm, MLA_WIDTH), row),
                  mod_spec, mod_spec, mod_spec, full(woa), full(wob), full(l1g), full(l1b),
                  full(wrh), full(wrl), full(br), full(wsg), full(wsu), full(wsd), full(ltri)],
        out_specs=[pl.BlockSpec((tm, d), row), pl.BlockSpec((tm, d), row), pl.BlockSpec((tm, d), row),
                   pl.BlockSpec((tm, N_EXPERTS), row), pl.BlockSpec((tm, N_EXPERTS), row),
                   pl.BlockSpec((1, 1, N_EXPERTS), lambda i: (i // tpc, 0, 0))],
        out_shape=[jax.ShapeDtypeStruct((m, d), F32), jax.ShapeDtypeStruct((m, d), BF16),
                   jax.ShapeDtypeStruct((m, d), F32), jax.ShapeDtypeStruct((m, N_EXPERTS), F32),
                   jax.ShapeDtypeStruct((m, N_EXPERTS), F32), jax.ShapeDtypeStruct((m // chunk, 1, N_EXPERTS), I32)],
        scratch_shapes=[pltpu.VMEM((1, N_EXPERTS), F32)],
        compiler_params=_cparams(("arbitrary",)),
        name="mid_stage",
    )(x, mix_a, mix_b, g1, sh2, sc2, woa, wob, l1g, l1b, wrh, wrl, br, wsg, wsu, wsd, ltri)


MOE_ROWS = 128


def _moe_kernel(cnt_ref, rk_ref, wt_ref, h_ref, wg_ref, wu_ref, wd_ref, o_ref):
    c = pl.program_id(0)
    e = pl.program_id(1)
    chunk = h_ref.shape[0]

    @pl.when(e == 0)
    def _():
        o_ref[...] = jnp.zeros(o_ref.shape, F32)

    n = cnt_ref[c * N_EXPERTS + e]
    rk = rk_ref[0]
    wt = wt_ref[0]
    rowi = lax.broadcasted_iota(I32, (MOE_ROWS, chunk), 0).astype(F32)

    def sub(sb, carry):
        hit = rk == rowi + _scalar_f32(sb * MOE_ROWS)
        x = _dot(jnp.where(hit, 1.0, 0.0).astype(BF16), h_ref[...]).astype(BF16)
        act = (_silu(_dot(x, wg_ref[0])) * _dot(x, wu_ref[0])).astype(BF16)
        y = _dot(act, wd_ref[0]).astype(BF16)
        o_ref[...] += _dot_tn(jnp.where(hit, wt, 0.0).astype(BF16), y)
        return carry

    lax.fori_loop(0, (n + MOE_ROWS - 1) // MOE_ROWS, sub, 0)


def _moe_stage(h2, wd, rk, cnt, wg, wu, wdn, chunk):
    m, d = h2.shape
    nc = m // chunk
    rk_t = rk.T.reshape(N_EXPERTS, 1, m)
    wd_t = wd.T.reshape(N_EXPERTS, 1, m)
    return pl.pallas_call(
        _moe_kernel,
        grid_spec=pltpu.PrefetchScalarGridSpec(
            num_scalar_prefetch=1, grid=(nc, N_EXPERTS),
            in_specs=[pl.BlockSpec((1, 1, chunk), lambda c, e, cnt: (e, 0, c)),
                      pl.BlockSpec((1, 1, chunk), lambda c, e, cnt: (e, 0, c)),
                      pl.BlockSpec((chunk, d), lambda c, e, cnt: (c, 0)),
                      pl.BlockSpec((1, d, EXPERT_DIM), lambda c, e, cnt: (e, 0, 0)),
                      pl.BlockSpec((1, d, EXPERT_DIM), lambda c, e, cnt: (e, 0, 0)),
                      pl.BlockSpec((1, EXPERT_DIM, d), lambda c, e, cnt: (e, 0, 0))],
            out_specs=pl.BlockSpec((chunk, d), lambda c, e, cnt: (c, 0))),
        out_shape=jax.ShapeDtypeStruct((m, d), F32),
        compiler_params=_cparams(("arbitrary", "arbitrary")),
        name="moe_routed",
    )(cnt.reshape(-1), rk_t, wd_t, h2, wg, wu, wdn)


def _final_kernel(x1_ref, moe_ref, shd_ref, g2_ref, l2g_ref, l2b_ref, o_ref):
    y = DEEPNORM_ALPHA * x1_ref[...] + g2_ref[...] * (moe_ref[...] + shd_ref[...])
    o_ref[...] = _ln_plain(y) * l2g_ref[...] + l2b_ref[...]


def _final_stage(x1, moe, shd, g2, l2g, l2b):
    m, d = x1.shape
    tm = TOKEN_TILE
    row = lambda i: (i, 0)
    fix = lambda i: (0, 0)
    mod_spec = pl.BlockSpec((tm, d), row) if g2.shape[0] == m else pl.BlockSpec((1, d), fix)
    tile = pl.BlockSpec((tm, d), row)
    return pl.pallas_call(
        _final_kernel,
        grid=(m // tm,),
        in_specs=[tile, tile, tile, mod_spec, pl.BlockSpec((1, d), fix), pl.BlockSpec((1, d), fix)],
        out_specs=tile,
        out_shape=jax.ShapeDtypeStruct((m, d), F32),
        compiler_params=_cparams(("arbitrary",)),
        name="final_stage",
    )(x1, moe, shd, g2, l2g, l2b)


PROMPT_CHUNK = 512


def _ffn_half(x, mix_a, mix_b, g1, sh2, sc2, g2, mid_wts, moe_wts, ln2, chunk):
    x1, h2, shd, wd, rk, cnt = _mid_stage(x, mix_a, mix_b, g1, sh2, sc2, mid_wts, chunk)
    moe = _moe_stage(h2, wd, rk, cnt, *moe_wts, chunk)
    return _final_stage(x1, moe, shd, g2, *ln2)


def kernel(x_prompt, x_sample, cache_moba_k, cache_moba_v, cache_mla_ckv, cache_mla_krope, page_table,
           c_prompt, c_sample, w_ada, b_ada, w_in, q_norm_g, kv_norm_g, w_uq, w_uk, w_uv, w_o,
           ln1_g, ln1_b, w_router, b_router, w_gate, w_up, w_down, ws_gate, ws_up, ws_down, ln2_g, ln2_b):
    assert w_ada.shape[0] == DEPTH == 1
    l = 0
    bp, s, d = x_prompt.shape
    db, t, _ = x_sample.shape
    assert bp == 1
    n_pool, page = cache_moba_k.shape[1], cache_moba_k.shape[2]
    assert page == LANES
    n_pages = page_table.shape[1]
    past = n_pages * page
    ms = db * t

    nb_c = bp + db
    c_all = jnp.pad(jnp.concatenate([c_prompt, c_sample], 0), ((0, -nb_c % SUBLANES), (0, 0)))
    ada = _adaln(c_all, w_ada[l], b_ada[l])
    mods_p = [ada[0:1, j * d:(j + 1) * d] for j in range(6)]
    mods_s = [jnp.repeat(ada[bp:nb_c, j * d:(j + 1) * d], t, axis=0) for j in range(6)]

    in_wts = _prep_in_weights(w_in[l], w_uq[l], w_uk[l], w_uv[l])
    row2 = lambda v: v.reshape(1, -1)
    wr_hi = w_router[l].astype(BF16)
    wr_lo = (w_router[l] - wr_hi.astype(F32)).astype(BF16)
    mid_wts = (w_o[l, :MOBA_WIDTH].astype(BF16), w_o[l, MOBA_WIDTH:].astype(BF16), row2(ln1_g[l]), row2(ln1_b[l]),
               wr_hi, wr_lo, row2(b_router[l]), ws_gate[l].astype(BF16), ws_up[l].astype(BF16), ws_down[l].astype(BF16))
    moe_wts = (w_gate[l].astype(BF16), w_up[l].astype(BF16), w_down[l].astype(BF16))
    ln2 = (row2(ln2_g[l]), row2(ln2_b[l]))

    xp = x_prompt.reshape(s, d)
    (qaf, qab, ka, va, kab, vab, ckv_p, kr_p, qm, km, vm, kmean) = _in_stage(
        xp, mods_p[0], mods_p[1], jnp.arange(s), in_wts, q_norm_g[l], kv_norm_g[l])
    ids = _moba_gate_prompt(qaf, kmean.reshape(-1, MOBA_WIDTH))
    mix_a = _flash_prompt(qab, kab, vab, ids)
    mix_b = _flash_prompt(qm, km, vm)
    y_prompt = _ffn_half(xp, mix_a, mix_b, mods_p[2], mods_p[3], mods_p[4], mods_p[5], mid_wts, moe_wts, ln2,
                         min(PROMPT_CHUNK, s))

    xs = x_sample.reshape(ms, d)
    pos_s = past + (jnp.arange(ms) % t)
    (qaf_s, _, ka_s, va_s, _, _, ckv_s, kr_s, qm_s, _, _, _) = _in_stage(
        xs, mods_s[0], mods_s[1], pos_s, in_wts, q_norm_g[l], kv_norm_g[l])
    kT = cache_moba_k.transpose(0, 1, 3, 4, 2).reshape(DEPTH * n_pool, MOBA_HEADS, MOBA_HEAD_DIM, page)
    vT = cache_moba_v.transpose(0, 1, 3, 4, 2).reshape(DEPTH * n_pool, MOBA_HEADS, MOBA_HEAD_DIM, page)
    krT = cache_mla_krope.transpose(0, 1, 3, 2).reshape(DEPTH * n_pool, MLA_ROPE_DIM, page)
    ckv_pages = cache_mla_ckv.reshape(DEPTH * n_pool, page, MLA_KV_RANK)
    layer_off = l * n_pool
    q3 = qaf_s.reshape(db, t, MOBA_WIDTH)
    ids_s = _moba_dec_gate(q3, kT.reshape(DEPTH * n_pool, MOBA_WIDTH, page), page_table, layer_off)
    mix_as = _moba_dec_attn(q3, ka_s.reshape(db, t, -1), va_s.reshape(db, t, -1), kT, vT, page_table, ids_s, layer_off)
    mix_bs = _mla_dec(qm_s.reshape(db, t, -1), ckv_s.reshape(db, t, -1), kr_s.reshape(db, t, -1),
                      ckv_pages, krT, page_table, w_uk[l], w_uv[l], layer_off)
    y_sample = _ffn_half(xs, mix_as.reshape(ms, -1).astype(BF16), mix_bs.reshape(ms, -1).astype(BF16),
                         mods_s[2], mods_s[3], mods_s[4], mods_s[5], mid_wts, moe_wts, ln2, ms)

    hd = (MOBA_HEADS, MOBA_HEAD_DIM)
    return (y_prompt.reshape(bp, s, d), y_sample.reshape(db, t, d),
            ka.reshape(DEPTH, bp, s, *hd), va.reshape(DEPTH, bp, s, *hd),
            ckv_p.reshape(DEPTH, bp, s, MLA_KV_RANK), kr_p.reshape(DEPTH, bp, s, MLA_ROPE_DIM),
            ka_s.reshape(DEPTH, db, t, *hd), va_s.reshape(DEPTH, db, t, *hd),
            ckv_s.reshape(DEPTH, db, t, MLA_KV_RANK), kr_s.reshape(DEPTH, db, t, MLA_ROPE_DIM))
```

```python
import functools
import math

import jax
import jax.numpy as jnp
from jax import lax
from jax.experimental import pallas as pl
from jax.experimental.pallas import tpu as pltpu

F32 = jnp.float32
BF16 = jnp.bfloat16
I32 = jnp.int32

D_MODEL = 1024
MOBA_HEADS = 8
MOBA_HEAD_DIM = 64
MOBA_BLOCK = 256
MOBA_TOPK = 3
MLA_HEADS = 8
MLA_Q_RANK = 256
MLA_KV_RANK = 128
MLA_NOPE_DIM = 64
MLA_ROPE_DIM = 32
MLA_V_DIM = 64
ROPE_THETA = 10000.0
MOBA_WIDTH = MOBA_HEADS * MOBA_HEAD_DIM
MLA_WIDTH = MLA_HEADS * MLA_V_DIM
N_EXPERTS = 64
EXPERT_DIM = 256
TOP_K = 8
N_GROUPS = 8
TOPK_GROUPS = 4
SHARED_DIM = 256
ROUTED_SCALE = 2.5
DEPTH = 1
DEEPNORM_ALPHA = (2 * DEPTH) ** 0.25
LN_EPS = 1e-5
RMS_EPS = 1e-6

LANES = 128
HEAD_PAD = 128
MLA_QK_DIM = MLA_NOPE_DIM + MLA_ROPE_DIM
MOBA_SCALE = MOBA_HEAD_DIM ** -0.5
MLA_SCALE = MLA_QK_DIM ** -0.5
LOG2E = math.log2(math.e)
MOBA_QSCALE = MOBA_SCALE * LOG2E
MLA_QSCALE = MLA_SCALE * LOG2E
NEG_BIG = -1e30
VMEM_LIMIT = 56 * 1024 * 1024

TOKEN_TILE = 256
_C_Q, _C_K, _C_V = 0, MOBA_WIDTH, 2 * MOBA_WIDTH
_C_CQ = 3 * MOBA_WIDTH
_C_CKV = _C_CQ + MLA_Q_RANK
_C_KR = _C_CKV + MLA_KV_RANK
_C_KRS = _C_KR + LANES
IN_COLS_EXT = _C_KRS + LANES


def _cparams(sem, vmem=VMEM_LIMIT):
    return pltpu.CompilerParams(dimension_semantics=sem, vmem_limit_bytes=vmem)


def _ln_plain(x):
    mu = jnp.mean(x, -1, keepdims=True)
    xc = x - mu
    var = jnp.mean(xc * xc, -1, keepdims=True)
    return xc * lax.rsqrt(var + LN_EPS)


def _dot(a, b):
    return jnp.dot(a, b, preferred_element_type=F32)


def _dot_nt(a, b):
    return lax.dot_general(a, b, (((1,), (1,)), ((), ())), preferred_element_type=F32)


def _dot_tn(a, b):
    return lax.dot_general(a, b, (((0,), (0,)), ((), ())), preferred_element_type=F32)


def _scalar_f32(v):
    return jnp.asarray(v, I32).astype(F32)


def _split_bf16(x):
    hi = x.astype(BF16)
    lo = (x - hi.astype(F32)).astype(BF16)
    return hi, lo


def _ada_kernel(c_ref, w_ref, b_ref, o_ref):
    c = c_ref[...]
    s = c * jax.nn.sigmoid(c)
    o_ref[...] = _dot(s.astype(BF16), w_ref[...].astype(BF16)) + b_ref[...]


def _adaln(c, w_ada, b_ada):
    b, d = c.shape
    n = w_ada.shape[1]
    tn = 1024
    return pl.pallas_call(
        _ada_kernel,
        grid=(n // tn,),
        in_specs=[pl.BlockSpec((b, d), lambda j: (0, 0)),
                  pl.BlockSpec((d, tn), lambda j: (0, j)),
                  pl.BlockSpec((1, tn), lambda j: (0, j))],
        out_specs=pl.BlockSpec((b, tn), lambda j: (0, j)),
        out_shape=jax.ShapeDtypeStruct((b, n), F32),
        compiler_params=_cparams(("arbitrary",)),
        name="adaln",
    )(c, w_ada, b_ada.reshape(1, n))


def _in_kernel(x_ref, sh_ref, sc_ref, win_ref, qg_ref, kvg_ref, wqa_ref, wqb_ref, wuk_ref, wuv_ref,
               ta_ref, tb_ref, tkc_ref, tks_ref,
               qaf_ref, qab_ref, ka_ref, va_ref, kab_ref, vab_ref, ckv_ref, kr_ref,
               qm_ref, km_ref, vm_ref, kmean_ref):
    x = x_ref[...]
    h = _ln_plain(x) * (1.0 + sc_ref[...]) + sh_ref[...]
    u = _dot(h.astype(BF16), win_ref[...])
    q_a = u[:, _C_Q:_C_K]
    k_a = u[:, _C_K:_C_V]
    v_a = u[:, _C_V:_C_CQ]
    qaf_ref[...] = q_a
    qab_ref[...] = (q_a * MOBA_QSCALE).astype(BF16)
    ka_ref[...] = k_a
    va_ref[...] = v_a
    kab_ref[...] = k_a.astype(BF16)
    vab_ref[...] = v_a.astype(BF16)
    kmean_ref[0] = jnp.mean(k_a, axis=0, keepdims=True)

    cq = u[:, _C_CQ:_C_CKV]
    cq = cq * lax.rsqrt(jnp.mean(cq * cq, -1, keepdims=True) + RMS_EPS) * qg_ref[...]
    cqb = cq.astype(BF16)
    qa = _dot(cqb, wqa_ref[...])
    qb = _dot(cqb, wqb_ref[...])
    ta = ta_ref[...]
    tb = tb_ref[...]
    for hh in range(MLA_HEADS):
        sl = slice(hh * HEAD_PAD, (hh + 1) * HEAD_PAD)
        qm_ref[:, sl] = (qa[:, sl] * ta + qb[:, sl] * tb).astype(BF16)

    ckv = u[:, _C_CKV:_C_KR]
    ckv = ckv * lax.rsqrt(jnp.mean(ckv * ckv, -1, keepdims=True) + RMS_EPS) * kvg_ref[...]
    ckv_ref[...] = ckv
    ckvb = ckv.astype(BF16)
    kr = u[:, _C_KR:_C_KRS] * tkc_ref[...] + u[:, _C_KRS:IN_COLS_EXT] * tks_ref[...]
    kr_ref[...] = pltpu.roll(kr, LANES - MLA_NOPE_DIM, 1)[:, :MLA_ROPE_DIM]
    kn = _dot(ckvb, wuk_ref[...])
    for hh in range(MLA_HEADS):
        sl = slice(hh * HEAD_PAD, (hh + 1) * HEAD_PAD)
        km_ref[:, sl] = (kn[:, sl] + kr).astype(BF16)
    vm_ref[...] = _dot(ckvb, wuv_ref[...]).astype(BF16)


def _rope_tables(pos):
    half = MLA_ROPE_DIM // 2
    inv = jnp.exp(-math.log(ROPE_THETA) * jnp.arange(half, dtype=F32) * (2.0 / MLA_ROPE_DIM))
    ang = pos.astype(F32)[:, None] * inv[None, :]
    cos = jnp.cos(ang)
    sin = jnp.sin(ang)
    cos2 = jnp.concatenate([cos, cos], -1)
    sin2 = jnp.concatenate([sin, sin], -1)
    t = pos.shape[0]
    z32 = jnp.zeros((t, MLA_ROPE_DIM), F32)
    z64 = jnp.zeros((t, MLA_NOPE_DIM), F32)
    ones = jnp.ones((t, MLA_NOPE_DIM), F32)
    ta = jnp.concatenate([ones, cos2, z32], -1) * MLA_QSCALE
    tb = jnp.concatenate([z64, sin2, z32], -1) * MLA_QSCALE
    tkc = jnp.concatenate([z64, cos2, z32], -1)
    tks = jnp.concatenate([z64, sin2, z32], -1)
    return ta, tb, tkc, tks


def _rot_half_cols(w):
    half = MLA_ROPE_DIM // 2
    return jnp.concatenate([-w[..., half:], w[..., :half]], -1)


def _prep_in_weights(w_in, w_uq, w_uk, w_uv):
    d = w_in.shape[0]
    o5 = 3 * MOBA_WIDTH + MLA_Q_RANK + MLA_KV_RANK
    w_kr = w_in[:, o5:]
    z64 = jnp.zeros((d, MLA_NOPE_DIM), F32)
    z32 = jnp.zeros((d, MLA_ROPE_DIM), F32)
    win_ext = jnp.concatenate([w_in[:, :o5], z64, w_kr, z32, z64, _rot_half_cols(w_kr), z32], -1).astype(BF16)
    wq = w_uq.reshape(MLA_Q_RANK, MLA_HEADS, MLA_QK_DIM)
    zq = jnp.zeros((MLA_Q_RANK, MLA_HEADS, MLA_ROPE_DIM), F32)
    zn = jnp.zeros((MLA_Q_RANK, MLA_HEADS, MLA_NOPE_DIM), F32)
    wqa = jnp.concatenate([wq, zq], -1).reshape(MLA_Q_RANK, MLA_HEADS * HEAD_PAD).astype(BF16)
    wqb = jnp.concatenate([zn, _rot_half_cols(wq[..., MLA_NOPE_DIM:]), zq], -1)
    wqb = wqb.reshape(MLA_Q_RANK, MLA_HEADS * HEAD_PAD).astype(BF16)
    zk = jnp.zeros((MLA_KV_RANK, MLA_HEADS, HEAD_PAD - MLA_NOPE_DIM), F32)
    wuk = jnp.concatenate([w_uk, zk], -1).reshape(MLA_KV_RANK, MLA_HEADS * HEAD_PAD).astype(BF16)
    wuv = w_uv.reshape(MLA_KV_RANK, MLA_WIDTH).astype(BF16)
    return win_ext, wqa, wqb, wuk, wuv


def _in_stage(x, sh, sc, pos, weights, q_norm_g, kv_norm_g):
    win_ext, wqa, wqb, wuk, wuv = weights
    m, d = x.shape
    tm = TOKEN_TILE
    nt = m // tm
    ta, tb, tkc, tks = _rope_tables(pos)
    row = lambda i: (i, 0)
    fix = lambda i: (0, 0)
    mod_spec = pl.BlockSpec((tm, d), row) if sh.shape[0] == m else pl.BlockSpec((1, d), fix)
    full = lambda a: pl.BlockSpec(a.shape, fix)
    hw = MLA_HEADS * HEAD_PAD
    outs = [
        ((m, MOBA_WIDTH), F32), ((m, MOBA_WIDTH), BF16),
        ((m, MOBA_WIDTH), F32), ((m, MOBA_WIDTH), F32),
        ((m, MOBA_WIDTH), BF16), ((m, MOBA_WIDTH), BF16),
        ((m, MLA_KV_RANK), F32), ((m, MLA_ROPE_DIM), F32),
        ((m, hw), BF16), ((m, hw), BF16), ((m, MLA_WIDTH), BF16),
    ]
    out_shape = [jax.ShapeDtypeStruct(s, t) for s, t in outs] + [jax.ShapeDtypeStruct((nt, 1, MOBA_WIDTH), F32)]
    out_specs = [pl.BlockSpec((tm, s[1]), row) for s, _ in outs] + [pl.BlockSpec((1, 1, MOBA_WIDTH), lambda i: (i, 0, 0))]
    return pl.pallas_call(
        _in_kernel,
        grid=(nt,),
        in_specs=[pl.BlockSpec((tm, d), row), mod_spec, mod_spec, full(win_ext),
                  pl.BlockSpec((1, MLA_Q_RANK), fix), pl.BlockSpec((1, MLA_KV_RANK), fix),
                  full(wqa), full(wqb), full(wuk), full(wuv),
                  pl.BlockSpec((tm, LANES), row), pl.BlockSpec((tm, LANES), row),
                  pl.BlockSpec((tm, LANES), row), pl.BlockSpec((tm, LANES), row)],
        out_specs=out_specs,
        out_shape=out_shape,
        compiler_params=_cparams(("arbitrary",)),
        name="in_stage",
    )(x, sh, sc, win_ext, q_norm_g.reshape(1, -1), kv_norm_g.reshape(1, -1), wqa, wqb, wuk, wuv, ta, tb, tkc, tks)


def _dot3(a, b):
    ah, al = _split_bf16(a)
    bh, bl = _split_bf16(b)
    return _dot(ah, bh) + (_dot(ah, bl) + _dot(al, bh))


MAX_BLOCKS = 64
N_PAIRS = MOBA_HEADS // 2
IDS_LANES = 8


def _top3_ids(g, lanef, base):
    ids = []
    for _ in range(MOBA_TOPK):
        m = jnp.max(g, axis=1, keepdims=True)
        idx = jnp.min(jnp.where(g == m, lanef, 1e9), axis=1, keepdims=True)
        ids.append(jnp.where(m > -jnp.inf, idx - base, -1.0))
        g = jnp.where(lanef == idx, -jnp.inf, g)
    return ids


def _gate_kernel(q_ref, km_ref, ids_ref):
    b_own = pl.program_id(0)
    gate = _dot3(q_ref[...], km_ref[...])
    t = gate.shape[0]
    lane = lax.broadcasted_iota(I32, (t, LANES), 1)
    lanef = lane.astype(F32)
    out_lane = lax.broadcasted_iota(I32, (t, IDS_LANES), 1)
    eligible = (lane & (MAX_BLOCKS - 1)) < b_own
    for p in range(N_PAIRS):
        g2 = gate[:, p * LANES:(p + 1) * LANES]
        out = jnp.full((t, IDS_LANES), -1.0, F32)
        for hh in range(2):
            half = (lane < MAX_BLOCKS) if hh == 0 else (lane >= MAX_BLOCKS)
            g = jnp.where(eligible & half, g2, -jnp.inf)
            ids = _top3_ids(g, lanef, float(hh * MAX_BLOCKS))
            for j in range(MOBA_TOPK):
                out = jnp.where(out_lane == 4 * hh + j, ids[j], out)
        ids_ref[p] = out.astype(I32)


def _moba_gate_prompt(q_f32, kmean):
    s = q_f32.shape[0]
    nb = kmean.shape[0]
    assert nb <= MAX_BLOCKS and s == nb * MOBA_BLOCK
    km = kmean.reshape(nb, MOBA_HEADS, MOBA_HEAD_DIM).transpose(1, 2, 0)
    km = jnp.pad(km, ((0, 0), (0, 0), (0, MAX_BLOCKS - nb)))
    eye = jnp.eye(MOBA_HEADS, dtype=F32)
    kmd = (km[:, :, None, :] * eye[:, None, :, None]).reshape(MOBA_WIDTH, MOBA_HEADS * MAX_BLOCKS)
    tm = MOBA_BLOCK
    return pl.pallas_call(
        _gate_kernel,
        grid=(s // tm,),
        in_specs=[pl.BlockSpec((tm, MOBA_WIDTH), lambda i: (i, 0)),
                  pl.BlockSpec(kmd.shape, lambda i: (0, 0))],
        out_specs=pl.BlockSpec((N_PAIRS, tm, IDS_LANES), lambda i: (0, i, 0)),
        out_shape=jax.ShapeDtypeStruct((N_PAIRS, s, IDS_LANES), I32),
        compiler_params=_cparams(("arbitrary",)),
        name="moba_gate_prompt",
    )(q_f32, kmd)


ATT_TILE = MOBA_BLOCK
ATT_GROUP = 8


def _flash_kernel(*refs, moba):
    if moba:
        slopes_ref, qt_ref, k_ref, vt_ref, ids_ref, o_ref = refs[:6]
        scr = refs[6:]
        m_sc, l_sc, acc_sc, bias_sc = scr[0:2], scr[2:4], scr[4:6], scr[6:8]
    else:
        qt_ref, k_ref, vt_ref, o_ref = refs[:4]
        scr = refs[4:]
        m_sc, l_sc, acc_sc = scr[0:2], scr[2:4], scr[4:6]
    p_idx = pl.program_id(0)
    i = pl.program_id(1)
    tq = tk = ATT_TILE
    hv = LANES // 2
    kidx = lax.broadcasted_iota(I32, (tk, tq), 0)
    qidx = lax.broadcasted_iota(I32, (tk, tq), 1)
    for hh in range(2):
        m_sc[hh][...] = jnp.full((1, tq), NEG_BIG, F32)
        l_sc[hh][...] = jnp.zeros((1, tq), F32)
        acc_sc[hh][...] = jnp.zeros((hv, tq), F32)
    qt = qt_ref[...]
    if moba:
        rel = (kidx - qidx).astype(F32)
        slope = [slopes_ref[2 * p_idx] * LOG2E, slopes_ref[2 * p_idx + 1] * LOG2E]
        for hh in range(2):
            bias_sc[hh][...] = slope[hh] * rel
        qrow = lax.broadcasted_iota(I32, qt.shape, 0)
        zero = jnp.zeros_like(qt)
        qh = [jnp.where(qrow < hv, qt, zero), jnp.where(qrow < hv, zero, qt)]
        ids = ids_ref[0]
    else:
        qh = [qt[:HEAD_PAD], qt[HEAD_PAD:]]

    def step(n0, nb, diag):
        start = pl.multiple_of(n0 * tk, tk)
        k_t = k_ref[pl.ds(start, nb * tk), :]
        v_t = vt_ref[:, pl.ds(start, nb * tk)]
        s_heads = [_dot(k_t if moba else k_t[:, hh * HEAD_PAD:(hh + 1) * HEAD_PAD], qh[hh]) for hh in range(2)]
        p_heads, alphas = [], []
        for hh in range(2):
            s_all = s_heads[hh]
            m_prev = m_sc[hh][...]
            m_new = m_prev
            blocks = []
            for b in range(nb):
                s = s_all[b * tk:(b + 1) * tk]
                if moba:
                    s = s + bias_sc[hh][...]
                if diag:
                    s = jnp.where(kidx <= qidx, s, -jnp.inf)
                mt = jnp.max(s, axis=0, keepdims=True)
                if moba and not diag:
                    n = n0 + b
                    c_n = -(slope[hh] * float(tk)) * _scalar_f32(i - n)
                    sel = ids[4 * hh:4 * hh + 1] == n
                    for j in range(1, MOBA_TOPK):
                        sel = sel | (ids[4 * hh + j:4 * hh + j + 1] == n)
                    m_new = jnp.maximum(m_new, jnp.where(sel, mt + c_n, NEG_BIG))
                    blocks.append((s, sel, c_n))
                else:
                    m_new = jnp.maximum(m_new, mt)
                    blocks.append((s, None, None))
            prs = []
            l_add = jnp.zeros((1, tq), F32)
            for s, sel, c_n in blocks:
                shift = m_new if sel is None else jnp.where(sel, m_new - c_n, jnp.inf)
                pr = jnp.exp2(s - shift)
                l_add = l_add + jnp.sum(pr, axis=0, keepdims=True)
                prs.append(pr.astype(BF16))
            p_heads.append(prs[0] if nb == 1 else jnp.concatenate(prs, axis=0))
            alpha = jnp.exp2(m_prev - m_new)
            alphas.append(alpha)
            l_sc[hh][...] = alpha * l_sc[hh][...] + l_add
            m_sc[hh][...] = m_new
        pv = [_dot(v_t[hh * hv:(hh + 1) * hv], p_heads[hh]) for hh in range(2)]
        for hh in range(2):
            acc_sc[hh][...] = alphas[hh] * acc_sc[hh][...] + pv[hh]

    def group_body(g, carry):
        step(g * ATT_GROUP, ATT_GROUP, False)
        return carry

    def pair_body(r, carry):
        step((i // ATT_GROUP) * ATT_GROUP + 2 * r, 2, False)
        return carry

    lax.fori_loop(0, i // ATT_GROUP, group_body, 0)
    lax.fori_loop(0, (i % ATT_GROUP) // 2, pair_body, 0)

    @pl.when(i % 2 == 1)
    def _():
        step(i - 1, 1, False)

    step(i, 1, True)
    o_ref[:hv] = (acc_sc[0][...] / l_sc[0][...]).astype(o_ref.dtype)
    o_ref[hv:] = (acc_sc[1][...] / l_sc[1][...]).astype(o_ref.dtype)


def _alibi_slopes():
    return jnp.exp2(-8.0 * jnp.arange(1, MOBA_HEADS + 1, dtype=F32) / MOBA_HEADS)


def _flash_prompt(q, k, v, ids=None):
    moba = ids is not None
    s = q.shape[0]
    tq = ATT_TILE
    nq = s // tq
    qw = q.shape[1] // N_PAIRS
    qt = q.T
    vt = v.T
    scratch = ([pltpu.VMEM((1, tq), F32)] * 4 + [pltpu.VMEM((LANES // 2, tq), F32)] * 2)
    if moba:
        in_specs = [pl.BlockSpec((qw, tq), lambda p, i, sl: (p, i)),
                    pl.BlockSpec((s, qw), lambda p, i, sl: (0, p)),
                    pl.BlockSpec((LANES, s), lambda p, i, sl: (p, 0)),
                    pl.BlockSpec((1, IDS_LANES, tq), lambda p, i, sl: (p, 0, i))]
        out_spec = pl.BlockSpec((LANES, tq), lambda p, i, sl: (p, i))
        scratch = scratch + [pltpu.VMEM((tq, tq), F32)] * 2
        nsp = 1
        args = (_alibi_slopes(), qt, k, vt, ids.transpose(0, 2, 1))
    else:
        in_specs = [pl.BlockSpec((qw, tq), lambda p, i: (p, i)),
                    pl.BlockSpec((s, qw), lambda p, i: (0, p)),
                    pl.BlockSpec((LANES, s), lambda p, i: (p, 0))]
        out_spec = pl.BlockSpec((LANES, tq), lambda p, i: (p, i))
        nsp = 0
        args = (qt, k, vt)
    out_t = pl.pallas_call(
        functools.partial(_flash_kernel, moba=moba),
        grid_spec=pltpu.PrefetchScalarGridSpec(
            num_scalar_prefetch=nsp, grid=(N_PAIRS, nq), in_specs=in_specs, out_specs=out_spec,
            scratch_shapes=scratch),
        out_shape=jax.ShapeDtypeStruct((N_PAIRS * LANES, s), BF16),
        compiler_params=_cparams(("arbitrary", "arbitrary")),
        name="moba_prompt" if moba else "mla_prompt",
    )(*args)
    return out_t.T


GATE_PAGES = 8
GATE_SLOTS = 4
Q_ROWS = 32
PAGES_PER_BLOCK = MOBA_BLOCK // LANES


def _k_chunk_copies(pt_ref, k_hbm, buf, sem, g, slot, layer_off):
    base = g * GATE_PAGES
    return [pltpu.make_async_copy(k_hbm.at[pt_ref[base + j] + layer_off], buf.at[slot, j], sem.at[slot])
            for j in range(GATE_PAGES)]


def _moba_dec_gate_kernel(pt_ref, qblk_ref, k_hbm, ids_ref, buf, sem, ks_sc, *, n_pages, n_batch, layer_off):
    b = pl.program_id(0)
    n_chunks = n_pages // GATE_PAGES
    total = n_batch * n_chunks
    ahead = GATE_SLOTS - 1
    blocks_per_chunk = GATE_PAGES // PAGES_PER_BLOCK
    lane = lax.broadcasted_iota(I32, (MOBA_WIDTH, LANES), 1)

    @pl.when(b == 0)
    def _():
        for g0 in range(min(ahead, total)):
            for cp in _k_chunk_copies(pt_ref, k_hbm, buf, sem, g0, g0, layer_off):
                cp.start()

    ks_sc[...] = jnp.zeros(ks_sc.shape, F32)

    def chunk(c, carry):
        g = b * n_chunks + c
        slot = g % GATE_SLOTS
        for cp in _k_chunk_copies(pt_ref, k_hbm, buf, sem, g, slot, layer_off):
            cp.wait()

        @pl.when(g + ahead < total)
        def _():
            for cp in _k_chunk_copies(pt_ref, k_hbm, buf, sem, g + ahead, (g + ahead) % GATE_SLOTS, layer_off):
                cp.start()

        for jb in range(blocks_per_chunk):
            blk = buf[slot, jb * PAGES_PER_BLOCK]
            for pg in range(1, PAGES_PER_BLOCK):
                blk = blk + buf[slot, jb * PAGES_PER_BLOCK + pg]
            ksum = jnp.sum(blk, axis=1, keepdims=True)
            n = c * blocks_per_chunk + jb
            ks_sc[...] = jnp.where(lane == n, ksum, ks_sc[...])
        return carry

    lax.fori_loop(0, n_chunks, chunk, 0)
    kmean = ks_sc[...] * (1.0 / MOBA_BLOCK)
    gate = _dot3(qblk_ref[0], kmean)
    n_full = n_pages // PAGES_PER_BLOCK
    glane = lax.broadcasted_iota(I32, (Q_ROWS, LANES), 1)
    g = jnp.where(glane < n_full, gate, -jnp.inf)
    ids = _top3_ids(g, glane.astype(F32), 0.0)
    out_lane = lax.broadcasted_iota(I32, (Q_ROWS, IDS_LANES), 1)
    out = jnp.full((Q_ROWS, IDS_LANES), -1.0, F32)
    for j in range(MOBA_TOPK):
        out = jnp.where(out_lane == j, ids[j], out)
    ids_ref[0] = out.astype(I32)


def _q_blockdiag(q):
    db, t, _ = q.shape
    eye = jnp.eye(MOBA_HEADS, dtype=q.dtype)
    q4 = q.reshape(db, t, 1, MOBA_HEADS, MOBA_HEAD_DIM) * eye[None, None, :, :, None]
    return q4.reshape(db, t * MOBA_HEADS, MOBA_WIDTH)


def _moba_dec_gate(q, kT_pages, page_table, layer_off):
    db, t, _ = q.shape
    n_pages = page_table.shape[1]
    assert t * MOBA_HEADS == Q_ROWS and n_pages % GATE_PAGES == 0
    assert n_pages % PAGES_PER_BLOCK == 0 and MOBA_TOPK <= n_pages // PAGES_PER_BLOCK <= LANES
    kern = functools.partial(_moba_dec_gate_kernel, n_pages=n_pages, n_batch=db, layer_off=layer_off)
    return pl.pallas_call(
        kern,
        grid_spec=pltpu.PrefetchScalarGridSpec(
            num_scalar_prefetch=1, grid=(db,),
            in_specs=[pl.BlockSpec((1, Q_ROWS, MOBA_WIDTH), lambda b, pt: (b, 0, 0)),
                      pl.BlockSpec(memory_space=pl.ANY)],
            out_specs=pl.BlockSpec((1, Q_ROWS, IDS_LANES), lambda b, pt: (b, 0, 0)),
            scratch_shapes=[pltpu.VMEM((GATE_SLOTS, GATE_PAGES, MOBA_WIDTH, LANES), F32),
                            pltpu.SemaphoreType.DMA((GATE_SLOTS,)),
                            pltpu.VMEM((MOBA_WIDTH, LANES), F32)]),
        out_shape=jax.ShapeDtypeStruct((db, Q_ROWS, IDS_LANES), I32),
        compiler_params=_cparams(("arbitrary",)),
        name="moba_sample_gate",
    )(page_table.reshape(-1), _q_blockdiag(q), kT_pages)


SEL_PAGES = MOBA_TOPK * PAGES_PER_BLOCK


def _sel_copies(pt_ref, ids_ref, k_hbm, v_hbm, kbuf, vbuf, sem, b, slot, r, n_pages, layer_off):
    h = r % MOBA_HEADS
    cps = []
    for j in range(MOBA_TOPK):
        n = ids_ref[(b * Q_ROWS + r) * MOBA_TOPK + j]
        for pg in range(PAGES_PER_BLOCK):
            page = pt_ref[b * n_pages + n * PAGES_PER_BLOCK + pg] + layer_off
            dst = r * SEL_PAGES + j * PAGES_PER_BLOCK + pg
            cps.append(pltpu.make_async_copy(k_hbm.at[page, h], kbuf.at[slot, dst], sem.at[slot]))
            cps.append(pltpu.make_async_copy(v_hbm.at[page, h], vbuf.at[slot, dst], sem.at[slot]))
    return cps


def _moba_dec_attn_kernel(pt_ref, ids_ref, slopes_ref, qb_ref, kn_ref, vn_ref, k_hbm, v_hbm, o_ref,
                          kbuf, vbuf, sem, *, n_pages, n_batch, layer_off):
    b = pl.program_id(0)
    slot = b % 2
    past = n_pages * LANES

    def issue(bb, sl):
        def body(r, carry):
            for cp in _sel_copies(pt_ref, ids_ref, k_hbm, v_hbm, kbuf, vbuf, sem, bb, sl, r, n_pages, layer_off):
                cp.start()
            return carry
        lax.fori_loop(0, Q_ROWS, body, 0, unroll=4)

    @pl.when(b == 0)
    def _():
        issue(0, 0)

    @pl.when(b + 1 < n_batch)
    def _():
        issue(b + 1, 1 - slot)

    def wait_body(r, carry):
        for cp in _sel_copies(pt_ref, ids_ref, k_hbm, v_hbm, kbuf, vbuf, sem, b, slot, r, n_pages, layer_off):
            cp.wait()
        return carry
    lax.fori_loop(0, Q_ROWS, wait_body, 0, unroll=4)

    lane = lax.broadcasted_iota(I32, (1, LANES), 1)
    lanef = lane.astype(F32)
    out_lane = lax.broadcasted_iota(I32, (MOBA_HEAD_DIM, LANES), 1)

    def row(r, acc):
        t = r // MOBA_HEADS
        h = r % MOBA_HEADS
        slope = slopes_ref[h]
        qb = qb_ref[0, r]
        qpos = _scalar_f32(past + t)
        scores = []
        for j in range(MOBA_TOPK):
            n = ids_ref[(b * Q_ROWS + r) * MOBA_TOPK + j]
            for pg in range(PAGES_PER_BLOCK):
                kt = kbuf[slot, r * SEL_PAGES + j * PAGES_PER_BLOCK + pg]
                s = jnp.sum(qb * kt, axis=0, keepdims=True) * MOBA_SCALE
                kpos = _scalar_f32(n * MOBA_BLOCK + pg * LANES) + lanef
                scores.append(s - slope * (qpos - kpos))
        hrow = pl.multiple_of(h * MOBA_HEAD_DIM, MOBA_HEAD_DIM)
        kt_own = kn_ref[0, pl.ds(hrow, MOBA_HEAD_DIM), :]
        s_own = jnp.sum(qb * kt_own, axis=0, keepdims=True) * MOBA_SCALE
        s_own = s_own - slope * (_scalar_f32(t) - lanef)
        scores.append(jnp.where(lane <= t, s_own, -jnp.inf))
        m = scores[0]
        for s in scores[1:]:
            m = jnp.maximum(m, s)
        m = jnp.max(m, axis=1, keepdims=True)
        ps = [jnp.exp(s - m) for s in scores]
        l = ps[0]
        for p_ in ps[1:]:
            l = l + p_
        l = jnp.sum(l, axis=1, keepdims=True)
        o = jnp.zeros((MOBA_HEAD_DIM, 1), F32)
        for idx in range(SEL_PAGES):
            vt = vbuf[slot, r * SEL_PAGES + idx]
            o = o + jnp.sum(vt * ps[idx], axis=1, keepdims=True)
        vt_own = vn_ref[0, pl.ds(hrow, MOBA_HEAD_DIM), :]
        o = o + jnp.sum(vt_own * ps[SEL_PAGES], axis=1, keepdims=True)
        o = o / l
        return jnp.where(out_lane == r, o, acc)

    acc = lax.fori_loop(0, Q_ROWS, row, jnp.zeros((MOBA_HEAD_DIM, LANES), F32), unroll=4)
    o_ref[0] = jnp.transpose(acc)[:Q_ROWS, :]


def _moba_dec_attn(q, k_new, v_new, kT_heads, vT_heads, page_table, ids, layer_off):
    db, t, _ = q.shape
    n_pages = page_table.shape[1]
    qrows = q.reshape(db, Q_ROWS, MOBA_HEAD_DIM)
    qb = jnp.broadcast_to(qrows[..., None], (db, Q_ROWS, MOBA_HEAD_DIM, LANES))
    pad_t = lambda a: jnp.pad(a.transpose(0, 2, 1), ((0, 0), (0, 0), (0, LANES - t)))
    ids_flat = ids[:, :, :MOBA_TOPK].reshape(-1)
    kern = functools.partial(_moba_dec_attn_kernel, n_pages=n_pages, n_batch=db, layer_off=layer_off)
    nbuf = Q_ROWS * SEL_PAGES
    out = pl.pallas_call(
        kern,
        grid_spec=pltpu.PrefetchScalarGridSpec(
            num_scalar_prefetch=3, grid=(db,),
            in_specs=[pl.BlockSpec((1, Q_ROWS, MOBA_HEAD_DIM, LANES), lambda b, *_: (b, 0, 0, 0)),
                      pl.BlockSpec((1, MOBA_WIDTH, LANES), lambda b, *_: (b, 0, 0)),
                      pl.BlockSpec((1, MOBA_WIDTH, LANES), lambda b, *_: (b, 0, 0)),
                      pl.BlockSpec(memory_space=pl.ANY), pl.BlockSpec(memory_space=pl.ANY)],
            out_specs=pl.BlockSpec((1, Q_ROWS, MOBA_HEAD_DIM), lambda b, *_: (b, 0, 0)),
            scratch_shapes=[pltpu.VMEM((2, nbuf, MOBA_HEAD_DIM, LANES), F32),
                            pltpu.VMEM((2, nbuf, MOBA_HEAD_DIM, LANES), F32),
                            pltpu.SemaphoreType.DMA((2,))]),
        out_shape=jax.ShapeDtypeStruct((db, Q_ROWS, MOBA_HEAD_DIM), F32),
        compiler_params=_cparams(("arbitrary",)),
        name="moba_sample_attn",
    )(page_table.reshape(-1), ids_flat, _alibi_slopes(), qb, pad_t(k_new), pad_t(v_new), kT_heads, vT_heads)
    return out.reshape(db, t, MOBA_WIDTH)


SUBLANES = 8
MLA_ROWS = MLA_HEADS * SUBLANES


def _mla_page_copies(pt_ref, ckv_hbm, kr_hbm, ckv_buf, kr_buf, sem, b, slot, p, n_pages, layer_off):
    page = pt_ref[b * n_pages + p] + layer_off
    off = pl.multiple_of(p * LANES, LANES)
    return [pltpu.make_async_copy(ckv_hbm.at[page], ckv_buf.at[slot, pl.ds(off, LANES), :], sem.at[slot]),
            pltpu.make_async_copy(kr_hbm.at[page], kr_buf.at[slot, :, pl.ds(off, LANES)], sem.at[slot])]


def _mla_dec_kernel(pt_ref, qs_ref, qr_ref, cn_ref, krn_ref, wuk_ref, wuv_ref, ckv_hbm, kr_hbm, o_ref,
                    ckv_buf, kr_buf, sem, *, n_pages, n_batch, layer_off):
    b = pl.program_id(0)
    slot = b % 2

    def issue(bb, sl):
        def body(p, carry):
            for cp in _mla_page_copies(pt_ref, ckv_hbm, kr_hbm, ckv_buf, kr_buf, sem, bb, sl, p, n_pages, layer_off):
                cp.start()
            return carry
        lax.fori_loop(0, n_pages, body, 0, unroll=8)

    @pl.when(b == 0)
    def _():
        issue(0, 0)

    @pl.when(b + 1 < n_batch)
    def _():
        issue(b + 1, 1 - slot)

    def wait_body(p, carry):
        for cp in _mla_page_copies(pt_ref, ckv_hbm, kr_hbm, ckv_buf, kr_buf, sem, b, slot, p, n_pages, layer_off):
            cp.wait()
        return carry
    lax.fori_loop(0, n_pages, wait_body, 0, unroll=8)

    qs = qs_ref[0]
    qr = qr_ref[0]
    q_lat = jnp.concatenate(
        [_dot(qs[h * SUBLANES:(h + 1) * SUBLANES], wuk_ref[h]) for h in range(MLA_HEADS)], axis=0).astype(BF16)
    ckv = ckv_buf[slot].astype(BF16)
    krt = kr_buf[slot].astype(BF16)
    cn = cn_ref[0].astype(BF16)
    krn = krn_ref[0].astype(BF16)
    s = _dot_nt(q_lat, ckv) + _dot(qr, krt)
    s_new = _dot_nt(q_lat, cn) + _dot_nt(qr, krn)
    t_row = lax.broadcasted_iota(I32, (MLA_ROWS, SUBLANES), 0) % SUBLANES
    t_col = lax.broadcasted_iota(I32, (MLA_ROWS, SUBLANES), 1)
    s_new = jnp.where(t_col <= t_row, s_new, -jnp.inf)
    m = jnp.maximum(jnp.max(s, axis=1, keepdims=True), jnp.max(s_new, axis=1, keepdims=True))
    p = jnp.exp2(s - m)
    p_new = jnp.exp2(s_new - m)
    l = jnp.sum(p, axis=1, keepdims=True) + jnp.sum(p_new, axis=1, keepdims=True)
    o_lat = (_dot(p.astype(BF16), ckv) + _dot(p_new.astype(BF16), cn)) / l
    o_lat = o_lat.astype(BF16)
    out = _dot(o_lat[:SUBLANES], wuv_ref[0])
    for h in range(1, MLA_HEADS):
        out = out + _dot(o_lat[h * SUBLANES:(h + 1) * SUBLANES], wuv_ref[h])
    o_ref[0] = out


def _mla_dec(q_mla, ckv_new, kr_new, ckv_pages, krT_pages, page_table, w_uk, w_uv, layer_off):
    db, t, _ = q_mla.shape
    n_pages = page_table.shape[1]
    past = n_pages * LANES
    assert t <= SUBLANES
    pad_rows = lambda a: jnp.pad(a, ((0, 0), (0, SUBLANES - t), (0, 0)))
    qs = pad_rows(q_mla).reshape(db, SUBLANES, MLA_HEADS, HEAD_PAD).transpose(0, 2, 1, 3).reshape(db, MLA_ROWS, HEAD_PAD)
    qr = qs[:, :, MLA_NOPE_DIM:MLA_QK_DIM]
    zk = jnp.zeros((MLA_HEADS, HEAD_PAD - MLA_NOPE_DIM, MLA_KV_RANK), F32)
    wuk_t = jnp.concatenate([w_uk.transpose(1, 2, 0), zk], axis=1).astype(BF16)
    eye = jnp.eye(MLA_HEADS, dtype=F32)
    wuv_e = (w_uv.transpose(1, 0, 2)[:, :, None, :] * eye[:, None, :, None]).reshape(MLA_HEADS, MLA_KV_RANK, MLA_WIDTH)
    wuv_e = wuv_e.astype(BF16)
    kern = functools.partial(_mla_dec_kernel, n_pages=n_pages, n_batch=db, layer_off=layer_off)
    per_b = lambda shp: pl.BlockSpec((1,) + shp, lambda b, pt: (b, 0, 0))
    out = pl.pallas_call(
        kern,
        grid_spec=pltpu.PrefetchScalarGridSpec(
            num_scalar_prefetch=1, grid=(db,),
            in_specs=[per_b((MLA_ROWS, HEAD_PAD)), per_b((MLA_ROWS, MLA_ROPE_DIM)),
                      per_b((SUBLANES, MLA_KV_RANK)), per_b((SUBLANES, MLA_ROPE_DIM)),
                      pl.BlockSpec(wuk_t.shape, lambda b, pt: (0, 0, 0)),
                      pl.BlockSpec(wuv_e.shape, lambda b, pt: (0, 0, 0)),
                      pl.BlockSpec(memory_space=pl.ANY), pl.BlockSpec(memory_space=pl.ANY)],
            out_specs=per_b((SUBLANES, MLA_WIDTH)),
            scratch_shapes=[pltpu.VMEM((2, past, MLA_KV_RANK), F32),
                            pltpu.VMEM((2, MLA_ROPE_DIM, past), F32),
                            pltpu.SemaphoreType.DMA((2,))]),
        out_shape=jax.ShapeDtypeStruct((db, SUBLANES, MLA_WIDTH), F32),
        compiler_params=_cparams(("arbitrary",)),
        name="mla_sample",
    )(page_table.reshape(-1), qs, qr, pad_rows(ckv_new), pad_rows(kr_new), wuk_t, wuv_e, ckv_pages, krT_pages)
    return out[:, :t, :]


GROUP_SIZE = N_EXPERTS // N_GROUPS


def _silu(x):
    return x * jax.nn.sigmoid(x)


def _route(sel, scores):
    t = sel.shape[0]
    lane = lax.broadcasted_iota(I32, (t, N_EXPERTS), 1)
    lanef = lane.astype(F32)
    grp = lane // GROUP_SIZE
    grpf = grp.astype(F32)
    gsc = jnp.zeros((t, N_EXPERTS), F32)
    for g in range(N_GROUPS):
        v = jnp.where(grp == g, sel, -jnp.inf)
        m1 = jnp.max(v, axis=1, keepdims=True)
        i1 = jnp.min(jnp.where(v == m1, lanef, 1e9), axis=1, keepdims=True)
        m2 = jnp.max(jnp.where(lanef == i1, -jnp.inf, v), axis=1, keepdims=True)
        gsc = jnp.where(grp == g, m1 + m2, gsc)
    emask = jnp.zeros((t, N_EXPERTS), jnp.bool_)
    for _ in range(TOPK_GROUPS):
        m = jnp.max(gsc, axis=1, keepdims=True)
        gi = jnp.min(jnp.where(gsc == m, grpf, 1e9), axis=1, keepdims=True)
        hit = grpf == gi
        emask = emask | hit
        gsc = jnp.where(hit, -jnp.inf, gsc)
    masked = jnp.where(emask, sel, -jnp.inf)
    chosen = jnp.zeros((t, N_EXPERTS), jnp.bool_)
    for _ in range(TOP_K):
        m = jnp.max(masked, axis=1, keepdims=True)
        ei = jnp.min(jnp.where(masked == m, lanef, 1e9), axis=1, keepdims=True)
        hit = lanef == ei
        chosen = chosen | hit
        masked = jnp.where(hit, -jnp.inf, masked)
    w = jnp.where(chosen, scores, 0.0)
    return w / jnp.sum(w, axis=1, keepdims=True) * ROUTED_SCALE, chosen


def _mid_kernel(x_ref, ma_ref, mb_ref, g1_ref, sh2_ref, sc2_ref, woa_ref, wob_ref, l1g_ref, l1b_ref,
                wrh_ref, wrl_ref, br_ref, wsg_ref, wsu_ref, wsd_ref, ltri_ref,
                x1_ref, h2_ref, shd_ref, wd_ref, rk_ref, cnt_ref, carry_sc, *, tiles_per_chunk):
    i = pl.program_id(0)
    a = _dot(ma_ref[...], woa_ref[...]) + _dot(mb_ref[...], wob_ref[...])
    y = DEEPNORM_ALPHA * x_ref[...] + g1_ref[...] * a
    x1 = _ln_plain(y) * l1g_ref[...] + l1b_ref[...]
    x1_ref[...] = x1
    h2 = _ln_plain(x1) * (1.0 + sc2_ref[...]) + sh2_ref[...]
    h2h, h2l = _split_bf16(h2)
    h2_ref[...] = h2h
    shd_ref[...] = _dot((_silu(_dot(h2h, wsg_ref[...])) * _dot(h2h, wsu_ref[...])).astype(BF16), wsd_ref[...])
    logits = _dot(h2h, wrh_ref[...]) + (_dot(h2h, wrl_ref[...]) + _dot(h2l, wrh_ref[...]))
    scores = jax.nn.sigmoid(logits)
    w, chosen = _route(scores + br_ref[...], scores)
    wd_ref[...] = w

    @pl.when(i % tiles_per_chunk == 0)
    def _():
        carry_sc[...] = jnp.zeros(carry_sc.shape, F32)

    onehot = jnp.where(chosen, 1.0, 0.0)
    before = _dot(ltri_ref[...], onehot.astype(BF16))
    carry = carry_sc[...]
    rk_ref[...] = jnp.where(chosen, carry + before, -1.0)
    carry = carry + jnp.sum(onehot, axis=0, keepdims=True)
    carry_sc[...] = carry
    cnt_ref[0] = carry.astype(I32)


def _mid_stage(x, mix_a, mix_b, g1, sh2, sc2, wts, chunk):
    woa, wob, l1g, l1b, wrh, wrl, br, wsg, wsu, wsd = wts
    m, d = x.shape
    tm = TOKEN_TILE
    nt = m // tm
    tpc = chunk // tm
    ltri = (jnp.arange(tm)[:, None] > jnp.arange(tm)[None, :]).astype(BF16)
    row = lambda i: (i, 0)
    fix = lambda i: (0, 0)
    mod_spec = pl.BlockSpec((tm, d), row) if g1.shape[0] == m else pl.BlockSpec((1, d), fix)
    full = lambda a: pl.BlockSpec(a.shape, fix)
    return pl.pallas_call(
        functools.partial(_mid_kernel, tiles_per_chunk=tpc),
        grid=(nt,),
        in_specs=[pl.BlockSpec((tm, d), row), pl.BlockSpec((tm, MOBA_WIDTH), row), pl.BlockSpec((tm, MLA_WIDTH), row),
                  mod_spec, mod_spec, mod_spec, full(woa), full(wob), full(l1g), full(l1b),
                  full(wrh), full(wrl), full(br), full(wsg), full(wsu), full(wsd), full(ltri)],
        out_specs=[pl.BlockSpec((tm, d), row), pl.BlockSpec((tm, d), row), pl.BlockSpec((tm, d), row),
                   pl.BlockSpec((tm, N_EXPERTS), row), pl.BlockSpec((tm, N_EXPERTS), row),
                   pl.BlockSpec((1, 1, N_EXPERTS), lambda i: (i // tpc, 0, 0))],
        out_shape=[jax.ShapeDtypeStruct((m, d), F32), jax.ShapeDtypeStruct((m, d), BF16),
                   jax.ShapeDtypeStruct((m, d), F32), jax.ShapeDtypeStruct((m, N_EXPERTS), F32),
                   jax.ShapeDtypeStruct((m, N_EXPERTS), F32), jax.ShapeDtypeStruct((m // chunk, 1, N_EXPERTS), I32)],
        scratch_shapes=[pltpu.VMEM((1, N_EXPERTS), F32)],
        compiler_params=_cparams(("arbitrary",)),
        name="mid_stage",
    )(x, mix_a, mix_b, g1, sh2, sc2, woa, wob, l1g, l1b, wrh, wrl, br, wsg, wsu, wsd, ltri)


MOE_ROWS = 192


def _moe_kernel(cnt_ref, rk_ref, wt_ref, h_ref, wg_ref, wu_ref, wd_ref, o_ref):
    c = pl.program_id(0)
    e = pl.program_id(1)
    chunk = h_ref.shape[0]

    @pl.when(e == 0)
    def _():
        o_ref[...] = jnp.zeros(o_ref.shape, F32)

    n = cnt_ref[c * N_EXPERTS + e]
    rk = rk_ref[0]
    wt = wt_ref[0]
    rowi = lax.broadcasted_iota(I32, (MOE_ROWS, chunk), 0).astype(F32)

    def sub(sb, carry):
        hit = rk == rowi + _scalar_f32(sb * MOE_ROWS)
        x = _dot(jnp.where(hit, 1.0, 0.0).astype(BF16), h_ref[...]).astype(BF16)
        act = (_silu(_dot(x, wg_ref[0])) * _dot(x, wu_ref[0])).astype(BF16)
        y = _dot(act, wd_ref[0]).astype(BF16)
        o_ref[...] += _dot_tn(jnp.where(hit, wt, 0.0).astype(BF16), y)
        return carry

    lax.fori_loop(0, (n + MOE_ROWS - 1) // MOE_ROWS, sub, 0)


def _moe_stage(h2, wd, rk, cnt, wg, wu, wdn, chunk):
    m, d = h2.shape
    nc = m // chunk
    rk_t = rk.T.reshape(N_EXPERTS, 1, m)
    wd_t = wd.T.reshape(N_EXPERTS, 1, m)
    return pl.pallas_call(
        _moe_kernel,
        grid_spec=pltpu.PrefetchScalarGridSpec(
            num_scalar_prefetch=1, grid=(nc, N_EXPERTS),
            in_specs=[pl.BlockSpec((1, 1, chunk), lambda c, e, cnt: (e, 0, c)),
                      pl.BlockSpec((1, 1, chunk), lambda c, e, cnt: (e, 0, c)),
                      pl.BlockSpec((chunk, d), lambda c, e, cnt: (c, 0)),
                      pl.BlockSpec((1, d, EXPERT_DIM), lambda c, e, cnt: (e, 0, 0)),
                      pl.BlockSpec((1, d, EXPERT_DIM), lambda c, e, cnt: (e, 0, 0)),
                      pl.BlockSpec((1, EXPERT_DIM, d), lambda c, e, cnt: (e, 0, 0))],
            out_specs=pl.BlockSpec((chunk, d), lambda c, e, cnt: (c, 0))),
        out_shape=jax.ShapeDtypeStruct((m, d), F32),
        compiler_params=_cparams(("arbitrary", "arbitrary")),
        name="moe_routed",
    )(cnt.reshape(-1), rk_t, wd_t, h2, wg, wu, wdn)


def _final_kernel(x1_ref, moe_ref, shd_ref, g2_ref, l2g_ref, l2b_ref, o_ref):
    y = DEEPNORM_ALPHA * x1_ref[...] + g2_ref[...] * (moe_ref[...] + shd_ref[...])
    o_ref[...] = _ln_plain(y) * l2g_ref[...] + l2b_ref[...]


def _final_stage(x1, moe, shd, g2, l2g, l2b):
    m, d = x1.shape
    tm = TOKEN_TILE
    row = lambda i: (i, 0)
    fix = lambda i: (0, 0)
    mod_spec = pl.BlockSpec((tm, d), row) if g2.shape[0] == m else pl.BlockSpec((1, d), fix)
    tile = pl.BlockSpec((tm, d), row)
    return pl.pallas_call(
        _final_kernel,
        grid=(m // tm,),
        in_specs=[tile, tile, tile, mod_spec, pl.BlockSpec((1, d), fix), pl.BlockSpec((1, d), fix)],
        out_specs=tile,
        out_shape=jax.ShapeDtypeStruct((m, d), F32),
        compiler_params=_cparams(("arbitrary",)),
        name="final_stage",
    )(x1, moe, shd, g2, l2g, l2b)


PROMPT_CHUNK = 1024


def _ffn_half(x, mix_a, mix_b, g1, sh2, sc2, g2, mid_wts, moe_wts, ln2, chunk):
    x1, h2, shd, wd, rk, cnt = _mid_stage(x, mix_a, mix_b, g1, sh2, sc2, mid_wts, chunk)
    moe = _moe_stage(h2, wd, rk, cnt, *moe_wts, chunk)
    return _final_stage(x1, moe, shd, g2, *ln2)


def kernel(x_prompt, x_sample, cache_moba_k, cache_moba_v, cache_mla_ckv, cache_mla_krope, page_table,
           c_prompt, c_sample, w_ada, b_ada, w_in, q_norm_g, kv_norm_g, w_uq, w_uk, w_uv, w_o,
           ln1_g, ln1_b, w_router, b_router, w_gate, w_up, w_down, ws_gate, ws_up, ws_down, ln2_g, ln2_b):
    assert w_ada.shape[0] == DEPTH == 1
    l = 0
    bp, s, d = x_prompt.shape
    db, t, _ = x_sample.shape
    assert bp == 1
    n_pool, page = cache_moba_k.shape[1], cache_moba_k.shape[2]
    assert page == LANES
    n_pages = page_table.shape[1]
    past = n_pages * page
    ms = db * t

    nb_c = bp + db
    c_all = jnp.pad(jnp.concatenate([c_prompt, c_sample], 0), ((0, -nb_c % SUBLANES), (0, 0)))
    ada = _adaln(c_all, w_ada[l], b_ada[l])
    mods_p = [ada[0:1, j * d:(j + 1) * d] for j in range(6)]
    mods_s = [jnp.repeat(ada[bp:nb_c, j * d:(j + 1) * d], t, axis=0) for j in range(6)]

    in_wts = _prep_in_weights(w_in[l], w_uq[l], w_uk[l], w_uv[l])
    row2 = lambda v: v.reshape(1, -1)
    wr_hi = w_router[l].astype(BF16)
    wr_lo = (w_router[l] - wr_hi.astype(F32)).astype(BF16)
    mid_wts = (w_o[l, :MOBA_WIDTH].astype(BF16), w_o[l, MOBA_WIDTH:].astype(BF16), row2(ln1_g[l]), row2(ln1_b[l]),
               wr_hi, wr_lo, row2(b_router[l]), ws_gate[l].astype(BF16), ws_up[l].astype(BF16), ws_down[l].astype(BF16))
    moe_wts = (w_gate[l].astype(BF16), w_up[l].astype(BF16), w_down[l].astype(BF16))
    ln2 = (row2(ln2_g[l]), row2(ln2_b[l]))

    xp = x_prompt.reshape(s, d)
    (qaf, qab, ka, va, kab, vab, ckv_p, kr_p, qm, km, vm, kmean) = _in_stage(
        xp, mods_p[0], mods_p[1], jnp.arange(s), in_wts, q_norm_g[l], kv_norm_g[l])
    ids = _moba_gate_prompt(qaf, kmean.reshape(-1, MOBA_WIDTH))
    mix_a = _flash_prompt(qab, kab, vab, ids)
    mix_b = _flash_prompt(qm, km, vm)
    y_prompt = _ffn_half(xp, mix_a, mix_b, mods_p[2], mods_p[3], mods_p[4], mods_p[5], mid_wts, moe_wts, ln2,
                         min(PROMPT_CHUNK, s))

    xs = x_sample.reshape(ms, d)
    pos_s = past + (jnp.arange(ms) % t)
    (qaf_s, _, ka_s, va_s, _, _, ckv_s, kr_s, qm_s, _, _, _) = _in_stage(
        xs, mods_s[0], mods_s[1], pos_s, in_wts, q_norm_g[l], kv_norm_g[l])
    kT = cache_moba_k.transpose(0, 1, 3, 4, 2).reshape(DEPTH * n_pool, MOBA_HEADS, MOBA_HEAD_DIM, page)
    vT = cache_moba_v.transpose(0, 1, 3, 4, 2).reshape(DEPTH * n_pool, MOBA_HEADS, MOBA_HEAD_DIM, page)
    krT = cache_mla_krope.transpose(0, 1, 3, 2).reshape(DEPTH * n_pool, MLA_ROPE_DIM, page)
    ckv_pages = cache_mla_ckv.reshape(DEPTH * n_pool, page, MLA_KV_RANK)
    layer_off = l * n_pool
    q3 = qaf_s.reshape(db, t, MOBA_WIDTH)
    ids_s = _moba_dec_gate(q3, kT.reshape(DEPTH * n_pool, MOBA_WIDTH, page), page_table, layer_off)
    mix_as = _moba_dec_attn(q3, ka_s.reshape(db, t, -1), va_s.reshape(db, t, -1), kT, vT, page_table, ids_s, layer_off)
    mix_bs = _mla_dec(qm_s.reshape(db, t, -1), ckv_s.reshape(db, t, -1), kr_s.reshape(db, t, -1),
                      ckv_pages, krT, page_table, w_uk[l], w_uv[l], layer_off)
    y_sample = _ffn_half(xs, mix_as.reshape(ms, -1).astype(BF16), mix_bs.reshape(ms, -1).astype(BF16),
                         mods_s[2], mods_s[3], mods_s[4], mods_s[5], mid_wts, moe_wts, ln2, ms)

    hd = (MOBA_HEADS, MOBA_HEAD_DIM)
    return (y_prompt.reshape(bp, s, d), y_sample.reshape(db, t, d),
            ka.reshape(DEPTH, bp, s, *hd), va.reshape(DEPTH, bp, s, *hd),
            ckv_p.reshape(DEPTH, bp, s, MLA_KV_RANK), kr_p.reshape(DEPTH, bp, s, MLA_ROPE_DIM),
            ka_s.reshape(DEPTH, db, t, *hd), va_s.reshape(DEPTH, db, t, *hd),
            ckv_s.reshape(DEPTH, db, t, MLA_KV_RANK), kr_s.reshape(DEPTH, db, t, MLA_ROPE_DIM))
```
